```python
import jax, jax.numpy as jnp
from jax import lax
import numpy as np

D_MODEL = 2048
BATCH = 2
SEQ = 4096
DEPTH = 2

N_META = 16
EPS = 1e-6
A_WIDTH = D_MODEL // 2
A_CONV = 3
B_WIDTH = D_MODEL // 2
B_WINDOWS = (2, 4, 8, 16)
B_GROUPS = len(B_WINDOWS)
B_GROUP_DIM = B_WIDTH // B_GROUPS
IN0_COLS = 3 * A_WIDTH + B_WIDTH
MIX0_WIDTH = A_WIDTH + B_WIDTH
C_WIDTH = D_MODEL
C_CONV = 31
D_FF = 5632
FFN_CONV = 3
N_EVEN = (DEPTH + 1) // 2
N_ODD = DEPTH // 2

kernel_name = "hybrid_shortconv_pool_conformer_trunk"


def rms_norm(x, g):
    xf = x.astype(jnp.float32)
    y = xf * lax.rsqrt(jnp.mean(xf * xf, axis=-1, keepdims=True) + EPS)
    return (y * g.astype(jnp.float32)).astype(x.dtype)


def layer_norm(x, g, b):
    xf = x.astype(jnp.float32)
    mu = jnp.mean(xf, axis=-1, keepdims=True)
    var = jnp.mean(jnp.square(xf - mu), axis=-1, keepdims=True)
    y = (xf - mu) * lax.rsqrt(var + EPS)
    return (y * g.astype(jnp.float32) + b.astype(jnp.float32)).astype(x.dtype)


def causal_dwconv(x, w):
    K, C = w.shape
    return lax.conv_general_dilated(
        x, w[:, None, :].astype(x.dtype), window_strides=(1,),
        padding=[(K - 1, 0)], dimension_numbers=('NWC', 'WIO', 'NWC'),
        feature_group_count=C)


def causal_multiscale_pool(v):
    S = v.shape[1]
    vf = v.astype(jnp.float32)
    cs = jnp.cumsum(vf, axis=1)
    t1 = jnp.arange(1, S + 1, dtype=jnp.float32)
    outs = []
    for g, w in enumerate(B_WINDOWS):
        cs_g = cs[:, :, g]
        prev = jnp.pad(cs_g, ((0, 0), (w, 0), (0, 0)))[:, :S]
        outs.append((cs_g - prev) / jnp.minimum(t1, float(w))[None, :, None])
    pooled = jnp.stack(outs, axis=2)
    return (pooled - vf).astype(v.dtype)


def even_mixer(h, w_in, conv_a, pool_w, pool_scale, w_out):
    u = jnp.einsum('bsd,dc->bsc', h, w_in)
    gate_b, gate_c, val_a, val_b = jnp.split(
        u, [A_WIDTH, 2 * A_WIDTH, 3 * A_WIDTH], axis=-1)
    y_a = gate_b * causal_dwconv(gate_c * val_a, conv_a)
    vg = val_b.reshape(val_b.shape[0], val_b.shape[1], B_GROUPS, B_GROUP_DIM)
    pg = causal_multiscale_pool(vg)
    y_b = jnp.einsum('bsgi,gio->bsgo', pg, pool_w).reshape(val_b.shape) * pool_scale
    y = jnp.concatenate([y_a, y_b], axis=-1)
    return jnp.einsum('bsc,cd->bsd', y, w_out)


def conformer_conv(h, w_pw1, b_pw1, w_dw, b_dw, ln_g, ln_b, w_pw2, b_pw2):
    u = jnp.einsum('bsd,dc->bsc', h, w_pw1) + b_pw1
    a, g = jnp.split(u, 2, axis=-1)
    u = a * jax.nn.sigmoid(g)
    u = causal_dwconv(u, w_dw) + b_dw
    u = jax.nn.silu(layer_norm(u, ln_g, ln_b))
    return jnp.einsum('bsc,cd->bsd', u, w_pw2) + b_pw2


def conv_ffn(h, w_up, conv_w, w_down):
    u = jnp.einsum('bsd,df->bsf', h, w_up)
    u = causal_dwconv(u, conv_w)
    g, v = jnp.split(u, 2, axis=-1)
    return jnp.einsum('bsf,fd->bsd', jax.nn.silu(g) * v, w_down)


def setup_inputs(seed: int = 0) -> dict:
    key = jax.random.key(seed)
    ks = jax.random.split(key, 24)
    f32 = jnp.float32
    nrm = lambda k, shape, scale: jax.random.normal(k, shape, f32) * scale
    gain = lambda k, shape: 1.0 + 0.05 * jax.random.normal(k, shape, f32)
    D = D_MODEL
    return {
        "x": nrm(ks[0], (BATCH, SEQ, D), 1.0),
        "meta_tokens": nrm(ks[1], (N_META, D), 1.0),
        "mix_pre_g": gain(ks[2], (DEPTH, D)),
        "mix_post_g": gain(ks[3], (DEPTH, D)),
        "ffn_pre_g": gain(ks[4], (DEPTH, D)),
        "ffn_post_g": gain(ks[5], (DEPTH, D)),
        "ab_w_in": nrm(ks[6], (N_EVEN, D, IN0_COLS), D ** -0.5),
        "ab_conv_w": nrm(ks[7], (N_EVEN, A_CONV, A_WIDTH), A_CONV ** -0.5),
        "ab_pool_w": nrm(ks[8], (N_EVEN, B_GROUPS, B_GROUP_DIM, B_GROUP_DIM), B_GROUP_DIM ** -0.5),
        "ab_pool_scale": gain(ks[9], (N_EVEN, B_WIDTH)),
        "ab_w_out": nrm(ks[10], (N_EVEN, MIX0_WIDTH, D), MIX0_WIDTH ** -0.5),
        "c_w_pw1": nrm(ks[11], (N_ODD, D, 2 * C_WIDTH), D ** -0.5),
        "c_b_pw1": nrm(ks[12], (N_ODD, 2 * C_WIDTH), 0.02),
        "c_w_dw": nrm(ks[13], (N_ODD, C_CONV, C_WIDTH), C_CONV ** -0.5),
        "c_b_dw": nrm(ks[14], (N_ODD, C_WIDTH), 0.02),
        "c_ln_g": gain(ks[15], (N_ODD, C_WIDTH)),
        "c_ln_b": nrm(ks[16], (N_ODD, C_WIDTH), 0.02),
        "c_w_pw2": nrm(ks[17], (N_ODD, C_WIDTH, D), C_WIDTH ** -0.5),
        "c_b_pw2": nrm(ks[18], (N_ODD, D), 0.02),
        "ffn_w_up": nrm(ks[19], (DEPTH, D, 2 * D_FF), D ** -0.5),
        "ffn_conv_w": nrm(ks[20], (DEPTH, FFN_CONV, 2 * D_FF), FFN_CONV ** -0.5),
        "ffn_w_down": nrm(ks[21], (DEPTH, D_FF, D), D_FF ** -0.5),
    }


def reference(x, meta_tokens, mix_pre_g, mix_post_g, ffn_pre_g, ffn_post_g,
              ab_w_in, ab_conv_w, ab_pool_w, ab_pool_scale, ab_w_out,
              c_w_pw1, c_b_pw1, c_w_dw, c_b_dw, c_ln_g, c_ln_b, c_w_pw2, c_b_pw2,
              ffn_w_up, ffn_conv_w, ffn_w_down):
    B = x.shape[0]
    meta = jnp.broadcast_to(meta_tokens.astype(x.dtype)[None], (B, N_META, x.shape[-1]))
    h = jnp.concatenate([meta, x], axis=1)
    for layer in range(DEPTH):
        z = rms_norm(h, mix_pre_g[layer])
        if layer % 2 == 0:
            i = layer // 2
            m = even_mixer(z, ab_w_in[i], ab_conv_w[i], ab_pool_w[i],
                           ab_pool_scale[i], ab_w_out[i])
        else:
            i = layer // 2
            m = conformer_conv(z, c_w_pw1[i], c_b_pw1[i], c_w_dw[i], c_b_dw[i],
                               c_ln_g[i], c_ln_b[i], c_w_pw2[i], c_b_pw2[i])
        h = h + rms_norm(m, mix_post_g[layer])
        z = rms_norm(h, ffn_pre_g[layer])
        f = conv_ffn(z, ffn_w_up[layer], ffn_conv_w[layer], ffn_w_down[layer])
        h = h + rms_norm(f, ffn_post_g[layer])
    return h[:, N_META:]
```

```python
import functools

import jax
import jax.numpy as jnp
from jax import lax
from jax.experimental import pallas as pl
from jax.experimental.pallas import tpu as pltpu

EPS = 1e-6
N_META = 16
A_WIDTH = 1024
B_WIDTH = 1024
B_WINDOWS = (2, 4, 8, 16)
B_GROUP_DIM = 256
C_CONV = 31
A_CONV = 3
FFN_CONV = 3

HIST_CONV3 = 8
HIST_POOL = 16
HIST_CONV31 = 32

MIX_CHUNK = 256
FFN_CHUNK = 512
CONF_CHUNK = 512

VMEM_LIMIT_BYTES = 56 * 1024 * 1024


def _rms(x, g):
    ms = jnp.mean(x * x, axis=-1, keepdims=True)
    return x * lax.rsqrt(ms + EPS) * g


def _sigmoid(x):
    return 1.0 / (1.0 + jnp.exp(-x))


def _dot(a, b):
    return jnp.dot(a, b, preferred_element_type=jnp.float32)


def _stage_rows(ext_ref, first, hist_ref, carry_ref, cur, hp, tm):
    @pl.when(first)
    def _():
        ext_ref[pl.ds(0, hp), :] = hist_ref[...]

    @pl.when(jnp.logical_not(first))
    def _():
        ext_ref[pl.ds(0, hp), :] = carry_ref[...]

    ext_ref[pl.ds(hp, tm), :] = cur
    tail = ext_ref[pl.ds(tm, hp), :]
    carry_ref[...] = tail
    return tail


def _causal_taps(ext_ref, w_ref, hp, tm, k):
    acc = None
    for q in range(k):
        term = w_ref[pl.ds(q, 1), :] * ext_ref[pl.ds(hp - (k - 1) + q, tm), :]
        acc = term if acc is None else acc + term
    return acc


def _mix0_kernel(tm, pos0, emit_state, *refs):
    (h_ref, gpre_ref, gpost_ref, wgb_ref, wgc_ref, wva_ref, wvb_ref, cw_ref,
     pw_ref, ps_ref, woa_ref, wob_ref, hc_ref, hv_ref) = refs[:14]
    if emit_state:
        out_ref, sc_ref, sv_ref = refs[14:17]
        z_ref, extc_ref, extv_ref, carc_ref, carv_ref = refs[17:]
    else:
        out_ref = refs[14]
        z_ref, extc_ref, extv_ref, carc_ref, carv_ref = refs[15:]
    i = pl.program_id(1)
    j = pl.program_id(2)
    nj = pl.num_programs(2)

    @pl.when(j == 0)
    def _():
        z_ref[...] = _rms(h_ref[...], gpre_ref[...]).astype(jnp.bfloat16)

    z = z_ref[...]
    first = i == 0

    c = _dot(z, wgc_ref[...]) * _dot(z, wva_ref[...])
    tail_c = _stage_rows(extc_ref, first, hc_ref, carc_ref.at[j], c, HIST_CONV3, tm)
    ya = _dot(z, wgb_ref[...]) * _causal_taps(extc_ref, cw_ref, HIST_CONV3, tm, A_CONV)

    vb = _dot(z, wvb_ref[...])
    tail_v = _stage_rows(extv_ref, first, hv_ref, carv_ref.at[j], vb, HIST_POOL, tm)
    window = jnp.left_shift(2, j)
    wsum = None
    for q in range(max(B_WINDOWS)):
        tap = jnp.where(q < window, 1.0, 0.0) * extv_ref[pl.ds(HIST_POOL - q, tm), :]
        wsum = tap if wsum is None else wsum + tap
    pos = pos0 + i * tm + lax.broadcasted_iota(jnp.int32, (tm, 1), 0)
    count = jnp.minimum(pos + 1, window).astype(jnp.float32)
    pg = wsum / count - vb
    yb = _dot(pg.astype(jnp.bfloat16), pw_ref[...]) * ps_ref[...]

    contrib = (_dot(ya.astype(jnp.bfloat16), woa_ref[...])
               + _dot(yb.astype(jnp.bfloat16), wob_ref[...]))

    @pl.when(j == 0)
    def _():
        out_ref[...] = contrib

    @pl.when(j > 0)
    def _():
        out_ref[...] += contrib

    if emit_state:
        sc_ref[...] = tail_c
        sv_ref[...] = tail_v

    @pl.when(j == nj - 1)
    def _():
        out_ref[...] = h_ref[...] + _rms(out_ref[...], gpost_ref[...])


def _mix0_call(h, gpre, gpost, w_in, conv_w, pool_w, pool_scale, w_out, hist_c, hist_v,
               *, tm, pos0, emit_state):
    b, s, d = h.shape
    ni = s // tm
    nj = A_WIDTH // MIX_CHUNK
    ck = MIX_CHUNK
    row = lambda bb, i, j: (bb, i, 0)
    fixed = lambda bb, i, j: (0, 0)
    in_specs = [
        pl.BlockSpec((None, tm, d), row),
        pl.BlockSpec((1, d), fixed),
        pl.BlockSpec((1, d), fixed),
        pl.BlockSpec((d, ck), lambda bb, i, j: (0, j)),
        pl.BlockSpec((d, ck), lambda bb, i, j: (0, nj + j)),
        pl.BlockSpec((d, ck), lambda bb, i, j: (0, 2 * nj + j)),
        pl.BlockSpec((d, ck), lambda bb, i, j: (0, 3 * nj + j)),
        pl.BlockSpec((A_CONV, ck), lambda bb, i, j: (0, j)),
        pl.BlockSpec((None, ck, ck), lambda bb, i, j: (j, 0, 0)),
        pl.BlockSpec((1, ck), lambda bb, i, j: (0, j)),
        pl.BlockSpec((ck, d), lambda bb, i, j: (j, 0)),
        pl.BlockSpec((ck, d), lambda bb, i, j: (nj + j, 0)),
        pl.BlockSpec((HIST_CONV3, ck), lambda bb, i, j: (0, j)),
        pl.BlockSpec((HIST_POOL, ck), lambda bb, i, j: (0, j)),
    ]
    out_shape = [jax.ShapeDtypeStruct((b, s, d), jnp.float32)]
    out_specs = [pl.BlockSpec((None, tm, d), row)]
    if emit_state:
        out_shape += [jax.ShapeDtypeStruct((HIST_CONV3, A_WIDTH), jnp.float32),
                      jax.ShapeDtypeStruct((HIST_POOL, B_WIDTH), jnp.float32)]
        out_specs += [pl.BlockSpec((HIST_CONV3, ck), lambda bb, i, j: (0, j)),
                      pl.BlockSpec((HIST_POOL, ck), lambda bb, i, j: (0, j))]
    scratch = [
        pltpu.VMEM((tm, d), jnp.bfloat16),
        pltpu.VMEM((tm + HIST_CONV3, ck), jnp.float32),
        pltpu.VMEM((tm + HIST_POOL, ck), jnp.float32),
        pltpu.VMEM((nj, HIST_CONV3, ck), jnp.float32),
        pltpu.VMEM((nj, HIST_POOL, ck), jnp.float32),
    ]
    res = pl.pallas_call(
        functools.partial(_mix0_kernel, tm, pos0, emit_state),
        grid=(b, ni, nj),
        in_specs=in_specs,
        out_specs=out_specs,
        out_shape=out_shape,
        scratch_shapes=scratch,
        compiler_params=pltpu.CompilerParams(
            dimension_semantics=("arbitrary", "arbitrary", "arbitrary"),
            vmem_limit_bytes=VMEM_LIMIT_BYTES),
        name=f"mix0_tm{tm}",
    )(h, gpre, gpost, w_in, w_in, w_in, w_in, conv_w, pool_w, pool_scale, w_out, w_out,
      hist_c, hist_v)
    return res


def _ffn_kernel(tm, emit_state, *refs):
    (h_ref, gpre_ref, gpost_ref, wg_ref, wv_ref, cwg_ref, cwv_ref, wd_ref,
     hg_ref, hv_ref) = refs[:10]
    if emit_state:
        out_ref, sg_ref, sv_ref = refs[10:13]
        z_ref, extg_ref, extv_ref, carg_ref, carv_ref = refs[13:]
    else:
        out_ref = refs[10]
        z_ref, extg_ref, extv_ref, carg_ref, carv_ref = refs[11:]
    i = pl.program_id(1)
    j = pl.program_id(2)
    nj = pl.num_programs(2)

    @pl.when(j == 0)
    def _():
        z_ref[...] = _rms(h_ref[...], gpre_ref[...]).astype(jnp.bfloat16)

    z = z_ref[...]
    first = i == 0

    ug = _dot(z, wg_ref[...])
    tail_g = _stage_rows(extg_ref, first, hg_ref, carg_ref.at[j], ug, HIST_CONV3, tm)
    uv = _dot(z, wv_ref[...])
    tail_v = _stage_rows(extv_ref, first, hv_ref, carv_ref.at[j], uv, HIST_CONV3, tm)

    g = _causal_taps(extg_ref, cwg_ref, HIST_CONV3, tm, FFN_CONV)
    v = _causal_taps(extv_ref, cwv_ref, HIST_CONV3, tm, FFN_CONV)
    act = (g * _sigmoid(g) * v).astype(jnp.bfloat16)
    contrib = _dot(act, wd_ref[...])

    @pl.when(j == 0)
    def _():
        out_ref[...] = contrib

    @pl.when(j > 0)
    def _():
        out_ref[...] += contrib

    if emit_state:
        sg_ref[...] = tail_g
        sv_ref[...] = tail_v

    @pl.when(j == nj - 1)
    def _():
        out_ref[...] = h_ref[...] + _rms(out_ref[...], gpost_ref[...])


def _ffn_call(h, gpre, gpost, w_up, conv_w, w_down, hist_g, hist_v, *, tm, emit_state):
    b, s, d = h.shape
    dff = w_down.shape[0]
    ni = s // tm
    tf = FFN_CHUNK
    nj = dff // tf
    row = lambda bb, i, j: (bb, i, 0)
    fixed = lambda bb, i, j: (0, 0)
    gcol = lambda bb, i, j: (0, j)
    vcol = lambda bb, i, j: (0, nj + j)
    in_specs = [
        pl.BlockSpec((None, tm, d), row),
        pl.BlockSpec((1, d), fixed),
        pl.BlockSpec((1, d), fixed),
        pl.BlockSpec((d, tf), gcol),
        pl.BlockSpec((d, tf), vcol),
        pl.BlockSpec((FFN_CONV, tf), gcol),
        pl.BlockSpec((FFN_CONV, tf), vcol),
        pl.BlockSpec((tf, d), lambda bb, i, j: (j, 0)),
        pl.BlockSpec((HIST_CONV3, tf), gcol),
        pl.BlockSpec((HIST_CONV3, tf), gcol),
    ]
    out_shape = [jax.ShapeDtypeStruct((b, s, d), jnp.float32)]
    out_specs = [pl.BlockSpec((None, tm, d), row)]
    if emit_state:
        out_shape += [jax.ShapeDtypeStruct((HIST_CONV3, dff), jnp.float32)] * 2
        out_specs += [pl.BlockSpec((HIST_CONV3, tf), gcol)] * 2
    scratch = [
        pltpu.VMEM((tm, d), jnp.bfloat16),
        pltpu.VMEM((tm + HIST_CONV3, tf), jnp.float32),
        pltpu.VMEM((tm + HIST_CONV3, tf), jnp.float32),
        pltpu.VMEM((nj, HIST_CONV3, tf), jnp.float32),
        pltpu.VMEM((nj, HIST_CONV3, tf), jnp.float32),
    ]
    return pl.pallas_call(
        functools.partial(_ffn_kernel, tm, emit_state),
        grid=(b, ni, nj),
        in_specs=in_specs,
        out_specs=out_specs,
        out_shape=out_shape,
        scratch_shapes=scratch,
        compiler_params=pltpu.CompilerParams(
            dimension_semantics=("arbitrary", "arbitrary", "arbitrary"),
            vmem_limit_bytes=VMEM_LIMIT_BYTES),
        name=f"ffn_tm{tm}",
    )(h, gpre, gpost, w_up, w_up, conv_w, conv_w, w_down, hist_g, hist_v)


def _conf_kernel(tm, n1, emit_state, *refs):
    (h_ref, gpre_ref, gpost_ref, wa_ref, wg_ref, ba_ref, bg_ref, wdw_ref, bdw_ref,
     lng_ref, lnb_ref, w2_ref, b2_ref, hist_ref) = refs[:14]
    if emit_state:
        out_ref, st_ref = refs[14:16]
        z_ref, ext_ref, car_ref, cbuf_ref = refs[16:]
    else:
        out_ref = refs[14]
        z_ref, ext_ref, car_ref, cbuf_ref = refs[15:]
    i = pl.program_id(1)
    j = pl.program_id(2)
    nj = pl.num_programs(2)
    cw = CONF_CHUNK

    @pl.when(j == 0)
    def _():
        z_ref[...] = _rms(h_ref[...], gpre_ref[...]).astype(jnp.bfloat16)

    @pl.when(j < n1)
    def _():
        z = z_ref[...]
        a = _dot(z, wa_ref[...]) + ba_ref[...]
        g = _dot(z, wg_ref[...]) + bg_ref[...]
        glu = a * _sigmoid(g)
        tail = _stage_rows(ext_ref, i == 0, hist_ref, car_ref.at[j], glu, HIST_CONV31, tm)
        if emit_state:
            st_ref[...] = tail
        conv = _causal_taps(ext_ref, wdw_ref, HIST_CONV31, tm, C_CONV) + bdw_ref[...]
        cbuf_ref[j] = conv

    @pl.when(j >= n1)
    def _():
        k = j - n1
        mean = None
        for q in range(n1):
            part = jnp.sum(cbuf_ref[q], axis=-1, keepdims=True)
            mean = part if mean is None else mean + part
        mean = mean / (n1 * cw)
        var = None
        for q in range(n1):
            dev = cbuf_ref[q] - mean
            part = jnp.sum(dev * dev, axis=-1, keepdims=True)
            var = part if var is None else var + part
        var = var / (n1 * cw)
        y = (cbuf_ref[k] - mean) * lax.rsqrt(var + EPS) * lng_ref[...] + lnb_ref[...]
        s = (y * _sigmoid(y)).astype(jnp.bfloat16)
        contrib = _dot(s, w2_ref[...])

        @pl.when(k == 0)
        def _():
            out_ref[...] = contrib + b2_ref[...]

        @pl.when(k > 0)
        def _():
            out_ref[...] += contrib

    @pl.when(j == nj - 1)
    def _():
        out_ref[...] = h_ref[...] + _rms(out_ref[...], gpost_ref[...])


def _conf_call(h, gpre, gpost, w1, b1, wdw, bdw, lng, lnb, w2, b2, hist,
               *, tm, emit_state):
    b, s, d = h.shape
    c = w2.shape[0]
    cw = CONF_CHUNK
    n1 = c // cw
    ni = s // tm
    row = lambda bb, i, j: (bb, i, 0)
    fixed = lambda bb, i, j: (0, 0)
    acol = lambda bb, i, j: (0, jnp.minimum(j, n1 - 1))
    gcol = lambda bb, i, j: (0, n1 + jnp.minimum(j, n1 - 1))
    kcol = lambda bb, i, j: (0, jnp.maximum(j - n1, 0))
    in_specs = [
        pl.BlockSpec((None, tm, d), row),
        pl.BlockSpec((1, d), fixed),
        pl.BlockSpec((1, d), fixed),
        pl.BlockSpec((d, cw), acol),
        pl.BlockSpec((d, cw), gcol),
        pl.BlockSpec((1, cw), acol),
        pl.BlockSpec((1, cw), gcol),
        pl.BlockSpec((C_CONV, cw), acol),
        pl.BlockSpec((1, cw), acol),
        pl.BlockSpec((1, cw), kcol),
        pl.BlockSpec((1, cw), kcol),
        pl.BlockSpec((cw, d), lambda bb, i, j: (jnp.maximum(j - n1, 0), 0)),
        pl.BlockSpec((1, d), fixed),
        pl.BlockSpec((HIST_CONV31, cw), acol),
    ]
    out_shape = [jax.ShapeDtypeStruct((b, s, d), jnp.float32)]
    out_specs = [pl.BlockSpec((None, tm, d), row)]
    if emit_state:
        out_shape += [jax.ShapeDtypeStruct((HIST_CONV31, c), jnp.float32)]
        out_specs += [pl.BlockSpec((HIST_CONV31, cw), acol)]
    scratch = [
        pltpu.VMEM((tm, d), jnp.bfloat16),
        pltpu.VMEM((tm + HIST_CONV31, cw), jnp.float32),
        pltpu.VMEM((n1, HIST_CONV31, cw), jnp.float32),
        pltpu.VMEM((n1, tm, cw), jnp.float32),
    ]
    return pl.pallas_call(
        functools.partial(_conf_kernel, tm, n1, emit_state),
        grid=(b, ni, 2 * n1),
        in_specs=in_specs,
        out_specs=out_specs,
        out_shape=out_shape,
        scratch_shapes=scratch,
        compiler_params=pltpu.CompilerParams(
            dimension_semantics=("arbitrary", "arbitrary", "arbitrary"),
            vmem_limit_bytes=VMEM_LIMIT_BYTES),
        name=f"conf_tm{tm}",
    )(h, gpre, gpost, w1, w1, b1, b1, wdw, bdw, lng, lnb, w2, b2, hist)


def kernel(x, meta_tokens, mix_pre_g, mix_post_g, ffn_pre_g, ffn_post_g, ab_w_in, ab_conv_w, ab_pool_w, ab_pool_scale, ab_w_out, c_w_pw1, c_b_pw1, c_w_dw, c_b_dw, c_ln_g, c_ln_b, c_w_pw2, c_b_pw2, ffn_w_up, ffn_conv_w, ffn_w_down):
    depth, d = mix_pre_g.shape
    dff = ffn_w_down.shape[1]
    bf = jnp.bfloat16
    f32 = jnp.float32
    tm_main = 512

    hm = meta_tokens.astype(x.dtype)[None]
    hx = x
    for layer in range(depth):
        gpre = mix_pre_g[layer][None]
        gpost = mix_post_g[layer][None]
        i = layer // 2
        if layer % 2 == 0:
            w_in = ab_w_in[i].astype(bf)
            w_out = ab_w_out[i].astype(bf)
            pool_w = ab_pool_w[i].astype(bf)
            conv_w = ab_conv_w[i]
            scale = ab_pool_scale[i][None]
            args = (gpre, gpost, w_in, conv_w, pool_w, scale, w_out)
            hm, st_c, st_v = _mix0_call(
                hm, *args, jnp.zeros((HIST_CONV3, A_WIDTH), f32),
                jnp.zeros((HIST_POOL, B_WIDTH), f32),
                tm=N_META, pos0=0, emit_state=True)
            (hx,) = _mix0_call(hx, *args, st_c, st_v,
                               tm=tm_main, pos0=N_META, emit_state=False)
        else:
            w1 = c_w_pw1[i].astype(bf)
            w2 = c_w_pw2[i].astype(bf)
            args = (gpre, gpost, w1, c_b_pw1[i][None], c_w_dw[i], c_b_dw[i][None],
                    c_ln_g[i][None], c_ln_b[i][None], w2, c_b_pw2[i][None])
            hm, st = _conf_call(hm, *args, jnp.zeros((HIST_CONV31, w2.shape[0]), f32),
                                tm=N_META, emit_state=True)
            (hx,) = _conf_call(hx, *args, st, tm=tm_main, emit_state=False)
        w_up = ffn_w_up[layer].astype(bf)
        w_down = ffn_w_down[layer].astype(bf)
        fargs = (ffn_pre_g[layer][None], ffn_post_g[layer][None], w_up,
                 ffn_conv_w[layer], w_down)
        zero_u = jnp.zeros((HIST_CONV3, dff), f32)
        hm, st_g, st_v = _ffn_call(hm, *fargs, zero_u, zero_u, tm=N_META, emit_state=True)
        (hx,) = _ffn_call(hx, *fargs, st_g, st_v, tm=tm_main, emit_state=False)
    return hx
```

```python
import functools

import jax
import jax.numpy as jnp
from jax import lax
from jax.experimental import pallas as pl
from jax.experimental.pallas import tpu as pltpu

EPS = 1e-6
N_META = 16
A_WIDTH = 1024
B_WIDTH = 1024
B_WINDOWS = (2, 4, 8, 16)
C_CONV = 31
A_CONV = 3
FFN_CONV = 3

SUBLANES = 8
HIST_CONV3 = 8
HIST_POOL = 16
HIST_CONV31 = 32

MIX_CHUNK = 256
FFN_CHUNK = 512
CONF_CHUNK = 512

VMEM_LIMIT_BYTES = 56 * 1024 * 1024


def _rms(x, g):
    ms = jnp.mean(x * x, axis=-1, keepdims=True)
    return x * lax.rsqrt(ms + EPS) * g


def _sigmoid(x):
    return 0.5 * jnp.tanh(0.5 * x) + 0.5


def _dot(a, b):
    return jnp.dot(a, b, preferred_element_type=jnp.float32)


def _with_history(carry_ref, cur, hp, tm):
    ext = jnp.concatenate([carry_ref[...], cur], axis=0)
    carry_ref[...] = ext[tm:tm + hp]
    return ext


def _shifted(ext):
    return [ext] + [pltpu.roll(ext, b, axis=0) for b in range(1, SUBLANES)]


def _causal_taps(ext, w_ref, hp, tm, k):
    delayed = _shifted(ext)
    acc = None
    for q in range(k):
        a, b = divmod(k - 1 - q, SUBLANES)
        lo = hp - SUBLANES * a
        term = w_ref[pl.ds(q, 1), :] * delayed[b][lo:lo + tm]
        acc = term if acc is None else acc + term
    return acc


def _start_tile(i, j, h_ref, gpre_ref, z_ref, carries):
    @pl.when(j == 0)
    def _():
        z_ref[...] = _rms(h_ref[...], gpre_ref[...]).astype(jnp.bfloat16)

    @pl.when(jnp.logical_and(i == 0, j == 0))
    def _():
        for hist_ref, carry_ref in carries:
            carry_ref[...] = hist_ref[...]


def _split_refs(refs, n_in, n_carry, emit_state):
    ins = refs[:n_in]
    out_ref = refs[n_in]
    if emit_state:
        carries = refs[n_in + 1:n_in + 1 + n_carry]
        scratch = refs[n_in + 1 + n_carry:]
    else:
        scratch = refs[n_in + 1:len(refs) - n_carry]
        carries = refs[len(refs) - n_carry:]
    return ins, out_ref, carries, scratch


def _call(body, grid, in_specs, args, out_struct, out_spec, carry_shapes, scratch,
          emit_state, name):
    out_shape = [out_struct]
    out_specs = [out_spec]
    scratch = list(scratch)
    if emit_state:
        for shp in carry_shapes:
            out_shape.append(jax.ShapeDtypeStruct(shp, jnp.float32))
            out_specs.append(pl.BlockSpec(shp, lambda bb, i, j, n=len(shp): (0,) * n))
    else:
        scratch += [pltpu.VMEM(shp, jnp.float32) for shp in carry_shapes]
    return pl.pallas_call(
        body, grid=grid, in_specs=in_specs, out_specs=out_specs, out_shape=out_shape,
        scratch_shapes=scratch,
        compiler_params=pltpu.CompilerParams(
            dimension_semantics=("arbitrary", "arbitrary", "arbitrary"),
            vmem_limit_bytes=VMEM_LIMIT_BYTES),
        name=name,
    )(*args)


def _full(shape):
    return pl.BlockSpec(shape, lambda bb, i, j, n=len(shape): (0,) * n)


def _mix0_kernel(tm, pos0, emit_state, *refs):
    ins, out_ref, (carc_ref, carv_ref), (z_ref,) = _split_refs(refs, 14, 2, emit_state)
    (h_ref, gpre_ref, gpost_ref, wgb_ref, wgc_ref, wva_ref, wvb_ref, cw_ref,
     pw_ref, ps_ref, woa_ref, wob_ref, hc_ref, hv_ref) = ins
    i = pl.program_id(1)
    j = pl.program_id(2)
    nj = pl.num_programs(2)
    _start_tile(i, j, h_ref, gpre_ref, z_ref, ((hc_ref, carc_ref), (hv_ref, carv_ref)))

    @pl.when(j == 0)
    def _():
        out_ref[...] = jnp.zeros_like(out_ref)

    z = z_ref[...]

    c = _dot(z, wgc_ref[...]) * _dot(z, wva_ref[...])
    ext_c = _with_history(carc_ref.at[j], c, HIST_CONV3, tm)
    ya = _dot(z, wgb_ref[...]) * _causal_taps(ext_c, cw_ref, HIST_CONV3, tm, A_CONV)

    vb = _dot(z, wvb_ref[...])
    s = _with_history(carv_ref.at[j], vb, HIST_POOL, tm)
    sums = []
    for step in (1, 2, 4, 8):
        s = s + pltpu.roll(s, step, axis=0)
        sums.append(s[HIST_POOL:HIST_POOL + tm])
    wsum = jnp.where(j == 0, sums[0],
                     jnp.where(j == 1, sums[1], jnp.where(j == 2, sums[2], sums[3])))
    window = jnp.left_shift(2, j)
    pos = pos0 + i * tm + lax.broadcasted_iota(jnp.int32, (tm, 1), 0)
    count = jnp.minimum(pos + 1, window).astype(jnp.float32)
    pg = wsum / count - vb
    yb = _dot(pg.astype(jnp.bfloat16), pw_ref[...]) * ps_ref[...]

    out_ref[...] += (_dot(ya.astype(jnp.bfloat16), woa_ref[...])
                     + _dot(yb.astype(jnp.bfloat16), wob_ref[...]))

    @pl.when(j == nj - 1)
    def _():
        out_ref[...] = h_ref[...] + _rms(out_ref[...], gpost_ref[...])


def _mix0_call(h, gpre, gpost, w_in, conv_w, pool_w, pool_scale, w_out, hist_c, hist_v,
               *, tm, pos0, emit_state):
    b, s, d = h.shape
    ck = MIX_CHUNK
    nj = A_WIDTH // ck
    assert B_WINDOWS == tuple(2 << g for g in range(nj)) and B_WIDTH == nj * ck
    row = lambda bb, i, j: (bb, i, 0)
    in_specs = [
        pl.BlockSpec((None, tm, d), row),
        _full((1, d)),
        _full((1, d)),
        pl.BlockSpec((d, ck), lambda bb, i, j: (0, j)),
        pl.BlockSpec((d, ck), lambda bb, i, j: (0, nj + j)),
        pl.BlockSpec((d, ck), lambda bb, i, j: (0, 2 * nj + j)),
        pl.BlockSpec((d, ck), lambda bb, i, j: (0, 3 * nj + j)),
        pl.BlockSpec((A_CONV, ck), lambda bb, i, j: (0, j)),
        pl.BlockSpec((None, ck, ck), lambda bb, i, j: (j, 0, 0)),
        pl.BlockSpec((1, ck), lambda bb, i, j: (0, j)),
        pl.BlockSpec((ck, d), lambda bb, i, j: (j, 0)),
        pl.BlockSpec((ck, d), lambda bb, i, j: (nj + j, 0)),
        _full((nj, HIST_CONV3, ck)),
        _full((nj, HIST_POOL, ck)),
    ]
    return _call(
        functools.partial(_mix0_kernel, tm, pos0, emit_state),
        (b, s // tm, nj), in_specs,
        (h, gpre, gpost, w_in, w_in, w_in, w_in, conv_w, pool_w, pool_scale, w_out, w_out,
         hist_c, hist_v),
        jax.ShapeDtypeStruct((b, s, d), jnp.float32), pl.BlockSpec((None, tm, d), row),
        [(nj, HIST_CONV3, ck), (nj, HIST_POOL, ck)],
        [pltpu.VMEM((tm, d), jnp.bfloat16)], emit_state, f"mix0_tm{tm}")


def _ffn_kernel(tm, emit_state, *refs):
    ins, out_ref, (carg_ref, carv_ref), (z_ref,) = _split_refs(refs, 10, 2, emit_state)
    (h_ref, gpre_ref, gpost_ref, wg_ref, wv_ref, cwg_ref, cwv_ref, wd_ref,
     hg_ref, hv_ref) = ins
    i = pl.program_id(1)
    j = pl.program_id(2)
    nj = pl.num_programs(2)
    _start_tile(i, j, h_ref, gpre_ref, z_ref, ((hg_ref, carg_ref), (hv_ref, carv_ref)))

    @pl.when(j == 0)
    def _():
        out_ref[...] = jnp.zeros_like(out_ref)

    z = z_ref[...]
    ext_g = _with_history(carg_ref.at[j], _dot(z, wg_ref[...]), HIST_CONV3, tm)
    ext_v = _with_history(carv_ref.at[j], _dot(z, wv_ref[...]), HIST_CONV3, tm)
    g = _causal_taps(ext_g, cwg_ref, HIST_CONV3, tm, FFN_CONV)
    v = _causal_taps(ext_v, cwv_ref, HIST_CONV3, tm, FFN_CONV)
    act = (g * _sigmoid(g) * v).astype(jnp.bfloat16)
    out_ref[...] += _dot(act, wd_ref[...])

    @pl.when(j == nj - 1)
    def _():
        out_ref[...] = h_ref[...] + _rms(out_ref[...], gpost_ref[...])


def _ffn_call(h, gpre, gpost, w_up, conv_w, w_down, hist_g, hist_v, *, layer, tm, emit_state):
    b, s, d = h.shape
    dff = w_down.shape[1]
    tf = FFN_CHUNK
    nj = dff // tf
    row = lambda bb, i, j: (bb, i, 0)
    gcol = lambda bb, i, j: (layer, 0, j)
    vcol = lambda bb, i, j: (layer, 0, nj + j)
    in_specs = [
        pl.BlockSpec((None, tm, d), row),
        _full((1, d)),
        _full((1, d)),
        pl.BlockSpec((None, d, tf), gcol),
        pl.BlockSpec((None, d, tf), vcol),
        pl.BlockSpec((None, FFN_CONV, tf), gcol),
        pl.BlockSpec((None, FFN_CONV, tf), vcol),
        pl.BlockSpec((None, tf, d), lambda bb, i, j: (layer, j, 0)),
        _full((nj, HIST_CONV3, tf)),
        _full((nj, HIST_CONV3, tf)),
    ]
    return _call(
        functools.partial(_ffn_kernel, tm, emit_state),
        (b, s // tm, nj), in_specs,
        (h, gpre, gpost, w_up, w_up, conv_w, conv_w, w_down, hist_g, hist_v),
        jax.ShapeDtypeStruct((b, s, d), jnp.float32), pl.BlockSpec((None, tm, d), row),
        [(nj, HIST_CONV3, tf)] * 2,
        [pltpu.VMEM((tm, d), jnp.bfloat16)], emit_state, f"ffn{layer}_tm{tm}")


def _conf_kernel(tm, n1, emit_state, *refs):
    ins, out_ref, (car_ref,), scratch = _split_refs(refs, 14, 1, emit_state)
    (h_ref, gpre_ref, gpost_ref, wa_ref, wg_ref, ba_ref, bg_ref, wdw_ref, bdw_ref,
     lng_ref, lnb_ref, w2_ref, b2_ref, hist_ref) = ins
    z_ref, cbuf_ref, mean_ref, rstd_ref = scratch
    i = pl.program_id(1)
    j = pl.program_id(2)
    nj = pl.num_programs(2)
    cw = CONF_CHUNK
    _start_tile(i, j, h_ref, gpre_ref, z_ref, ((hist_ref, car_ref),))

    @pl.when(j < n1)
    def _():
        z = z_ref[...]
        a = _dot(z, wa_ref[...]) + ba_ref[...]
        g = _dot(z, wg_ref[...]) + bg_ref[...]
        ext = _with_history(car_ref.at[j], a * _sigmoid(g), HIST_CONV31, tm)
        cbuf_ref[j] = _causal_taps(ext, wdw_ref, HIST_CONV31, tm, C_CONV) + bdw_ref[...]

    @pl.when(j == n1)
    def _():
        total = None
        for q in range(n1):
            part = jnp.sum(cbuf_ref[q], axis=-1, keepdims=True)
            total = part if total is None else total + part
        mean = total / (n1 * cw)
        total = None
        for q in range(n1):
            dev = cbuf_ref[q] - mean
            part = jnp.sum(dev * dev, axis=-1, keepdims=True)
            total = part if total is None else total + part
        mean_ref[...] = mean
        rstd_ref[...] = lax.rsqrt(total / (n1 * cw) + EPS)
        out_ref[...] = jnp.broadcast_to(b2_ref[...], out_ref.shape)

    @pl.when(j >= n1)
    def _():
        y = (cbuf_ref[j - n1] - mean_ref[...]) * rstd_ref[...] * lng_ref[...] + lnb_ref[...]
        s = (y * _sigmoid(y)).astype(jnp.bfloat16)
        out_ref[...] += _dot(s, w2_ref[...])

    @pl.when(j == nj - 1)
    def _():
        out_ref[...] = h_ref[...] + _rms(out_ref[...], gpost_ref[...])


def _conf_call(h, gpre, gpost, w1, b1, wdw, bdw, lng, lnb, w2, b2, hist,
               *, tm, emit_state):
    b, s, d = h.shape
    c = w2.shape[0]
    cw = CONF_CHUNK
    n1 = c // cw
    row = lambda bb, i, j: (bb, i, 0)
    acol = lambda bb, i, j: (0, jnp.minimum(j, n1 - 1))
    gcol = lambda bb, i, j: (0, n1 + jnp.minimum(j, n1 - 1))
    kcol = lambda bb, i, j: (0, jnp.maximum(j - n1, 0))
    in_specs = [
        pl.BlockSpec((None, tm, d), row),
        _full((1, d)),
        _full((1, d)),
        pl.BlockSpec((d, cw), acol),
        pl.BlockSpec((d, cw), gcol),
        pl.BlockSpec((1, cw), acol),
        pl.BlockSpec((1, cw), gcol),
        pl.BlockSpec((C_CONV, cw), acol),
        pl.BlockSpec((1, cw), acol),
        pl.BlockSpec((1, cw), kcol),
        pl.BlockSpec((1, cw), kcol),
        pl.BlockSpec((cw, d), lambda bb, i, j: (jnp.maximum(j - n1, 0), 0)),
        _full((1, d)),
        _full((n1, HIST_CONV31, cw)),
    ]
    scratch = [
        pltpu.VMEM((tm, d), jnp.bfloat16),
        pltpu.VMEM((n1, tm, cw), jnp.float32),
        pltpu.VMEM((tm, 1), jnp.float32),
        pltpu.VMEM((tm, 1), jnp.float32),
    ]
    return _call(
        functools.partial(_conf_kernel, tm, n1, emit_state),
        (b, s // tm, 2 * n1), in_specs,
        (h, gpre, gpost, w1, w1, b1, b1, wdw, bdw, lng, lnb, w2, b2, hist),
        jax.ShapeDtypeStruct((b, s, d), jnp.float32), pl.BlockSpec((None, tm, d), row),
        [(n1, HIST_CONV31, cw)], scratch, emit_state, f"conf_tm{tm}")


def kernel(x, meta_tokens, mix_pre_g, mix_post_g, ffn_pre_g, ffn_post_g, ab_w_in, ab_conv_w, ab_pool_w, ab_pool_scale, ab_w_out, c_w_pw1, c_b_pw1, c_w_dw, c_b_dw, c_ln_g, c_ln_b, c_w_pw2, c_b_pw2, ffn_w_up, ffn_conv_w, ffn_w_down):
    depth, d = mix_pre_g.shape
    dff = ffn_w_down.shape[1]
    bf = jnp.bfloat16
    f32 = jnp.float32
    tm_main = 512
    w_up = ffn_w_up.astype(bf)
    w_down = ffn_w_down.astype(bf)

    def zeros(hp, width, chunk):
        return jnp.zeros((width // chunk, hp, chunk), f32)

    hm = meta_tokens.astype(x.dtype)[None]
    hx = x
    for layer in range(depth):
        gpre = mix_pre_g[layer][None]
        gpost = mix_post_g[layer][None]
        i = layer // 2
        if layer % 2 == 0:
            args = (gpre, gpost, ab_w_in[i].astype(bf), ab_conv_w[i], ab_pool_w[i].astype(bf),
                    ab_pool_scale[i][None], ab_w_out[i].astype(bf))
            hm, st_c, st_v = _mix0_call(
                hm, *args, zeros(HIST_CONV3, A_WIDTH, MIX_CHUNK),
                zeros(HIST_POOL, B_WIDTH, MIX_CHUNK), tm=N_META, pos0=0, emit_state=True)
            (hx,) = _mix0_call(hx, *args, st_c, st_v,
                               tm=tm_main, pos0=N_META, emit_state=False)
        else:
            args = (gpre, gpost, c_w_pw1[i].astype(bf), c_b_pw1[i][None], c_w_dw[i],
                    c_b_dw[i][None], c_ln_g[i][None], c_ln_b[i][None],
                    c_w_pw2[i].astype(bf), c_b_pw2[i][None])
            hm, st = _conf_call(hm, *args, zeros(HIST_CONV31, c_w_pw2.shape[1], CONF_CHUNK),
                                tm=N_META, emit_state=True)
            (hx,) = _conf_call(hx, *args, st, tm=tm_main, emit_state=False)
        fargs = (ffn_pre_g[layer][None], ffn_post_g[layer][None], w_up, ffn_conv_w, w_down)
        zero_u = zeros(HIST_CONV3, dff, FFN_CHUNK)
        hm, st_g, st_v = _ffn_call(hm, *fargs, zero_u, zero_u,
                                   layer=layer, tm=N_META, emit_state=True)
        (hx,) = _ffn_call(hx, *fargs, st_g, st_v, layer=layer, tm=tm_main, emit_state=False)
    return hx
```

```python
import functools

import jax
import jax.numpy as jnp
from jax import lax
from jax.experimental import pallas as pl
from jax.experimental.pallas import tpu as pltpu

EPS = 1e-6
N_META = 16
A_WIDTH = 1024
B_WIDTH = 1024
B_WINDOWS = (2, 4, 8, 16)
C_CONV = 31
A_CONV = 3
FFN_CONV = 3

SUBLANES = 8
LANES = 128
CONV_ROWS = 64
GATE_ROWS = 32
HIST_CONV3 = 8
HIST_POOL = 16
HIST_CONV31 = 32

TM_MAIN = 512
MIX_CHUNK = 256
FFN_CHUNK = 512
CONF_CHUNK = 512

VMEM_LIMIT_BYTES = 56 * 1024 * 1024


def _rms(x, g):
    ms = jnp.mean(x * x, axis=-1, keepdims=True)
    return x * lax.rsqrt(ms + EPS) * g


def _sigmoid(x):
    return 0.5 * jnp.tanh(0.5 * x) + 0.5


def _dot(a, b):
    return jnp.dot(a, b, preferred_element_type=jnp.float32)


def _with_history(carry_ref, cur, hp, tm):
    ext = jnp.concatenate([carry_ref[...], cur], axis=0)
    carry_ref[...] = ext[tm:tm + hp]
    return ext


def _causal_taps(ext, w_ref, hp, tm, k):
    delayed = [ext] + [pltpu.roll(ext, b, axis=0) for b in range(1, min(k, SUBLANES))]
    acc = None
    for q in range(k):
        a, b = divmod(k - 1 - q, SUBLANES)
        lo = hp - SUBLANES * a
        term = w_ref[pl.ds(q, 1), :] * delayed[b][lo:lo + tm]
        acc = term if acc is None else acc + term
    return acc


def _conv3_block(u_ref, carry_ref, w_ref, r, rows, cols):
    if r == 0:
        win = jnp.concatenate([carry_ref[:, cols], u_ref[pl.ds(0, rows), cols]], axis=0)
    else:
        win = u_ref[pl.ds(r - SUBLANES, rows + SUBLANES), cols]
    acc = w_ref[pl.ds(2, 1), cols] * win[SUBLANES:]
    for delay in (1, 2):
        acc = acc + w_ref[pl.ds(2 - delay, 1), cols] * pltpu.roll(win, delay, axis=0)[SUBLANES:]
    return acc


def _causal_taps_blocked(ext, sh_ref, w_ref, b_ref, dst_ref, hp, tm, k):
    sh_ref[0] = ext
    for b in range(1, SUBLANES):
        sh_ref[b] = pltpu.roll(ext, b, axis=0)
    for c in range(0, dst_ref.shape[1], LANES):
        cols = pl.ds(c, LANES)
        for r in range(0, tm, CONV_ROWS):
            acc = b_ref[:, cols]
            for q in range(k):
                a, b = divmod(k - 1 - q, SUBLANES)
                lo = hp - SUBLANES * a + r
                acc = acc + w_ref[pl.ds(q, 1), cols] * sh_ref[b, pl.ds(lo, min(CONV_ROWS, tm)), cols]
            dst_ref[pl.ds(r, min(CONV_ROWS, tm)), cols] = acc


def _start_tile(i, j, h_ref, gpre_ref, z_ref, out_ref, carries):
    @pl.when(j == 0)
    def _():
        z_ref[...] = _rms(h_ref[...], gpre_ref[...]).astype(jnp.bfloat16)
        if out_ref is not None:
            out_ref[...] = jnp.zeros_like(out_ref)

    @pl.when(jnp.logical_and(i == 0, j == 0))
    def _():
        for hist_ref, carry_ref in carries:
            carry_ref[...] = hist_ref[...]


def _finish_tile(h_ref, gpost_ref, out_ref):
    out_ref[...] = h_ref[...] + _rms(out_ref[...], gpost_ref[...])


def _pipelined(j, n, produce, prepare, consume, finish):
    @pl.when(j == 0)
    def _():
        produce(0)

    for slot in (0, 1):
        @pl.when(jnp.logical_and(jnp.logical_and(j > 0, j < n), lax.rem(j, 2) == slot))
        def _(slot=slot):
            prepare(1 - slot, j - 1)
            produce(slot)
            consume()

    @pl.when(j == n)
    def _():
        prepare((n - 1) % 2, n - 1)
        consume()
        finish()


def _pipelined3(j, n, produce, prepare, consume, finish):
    assert n >= 3

    @pl.when(j == 0)
    def _():
        produce(0)

    @pl.when(j == 1)
    def _():
        prepare(0, 0)
        produce(1)

    for p in (0, 1):
        @pl.when(jnp.logical_and(jnp.logical_and(j > 1, j < n), lax.rem(j, 2) == p))
        def _(p=p):
            prepare(1 - p, j - 1)
            produce(p)
            consume(p)

    @pl.when(j == n)
    def _():
        prepare((n - 1) % 2, n - 1)
        consume(n % 2)

    @pl.when(j == n + 1)
    def _():
        consume((n - 1) % 2)
        finish()


def _flat_step():
    return ((pl.program_id(0) * pl.num_programs(1) + pl.program_id(1)) * pl.num_programs(2)
            + pl.program_id(2))


def _entry(body, n_in, cast_steps, n_carry, emit_state, *refs):
    n_cast = len(cast_steps)
    ins = refs[:n_in]
    cast_src = refs[n_in:n_in + n_cast]
    o = n_in + n_cast
    out_ref = refs[o]
    cast_dst = refs[o + 1:o + 1 + n_cast]
    o += 1 + n_cast
    if emit_state:
        carries, scratch = refs[o:o + n_carry], refs[o + n_carry:]
    else:
        scratch, carries = refs[o:len(refs) - n_carry], refs[len(refs) - n_carry:]
    step = _flat_step()
    for src, dst, n in zip(cast_src, cast_dst, cast_steps):
        @pl.when(step < n)
        def _(src=src, dst=dst):
            dst[...] = src[...].astype(jnp.bfloat16)
    body(ins, out_ref, carries, scratch)


def _cast_job(w, lead, rows):
    shape = w.shape if lead is None else w.shape[1:]
    n = shape[0] // rows
    assert n * rows == shape[0]

    def make(prefix, block):
        def index(bb, i, j, grid):
            step = (bb * grid[1] + i) * grid[2] + j
            return prefix + (jnp.minimum(step, n - 1), 0)
        return block, index

    src = make(() if lead is None else (lead,), ((rows, shape[1]) if lead is None
                                                 else (None, rows, shape[1])))
    dst = make((), (rows, shape[1]))
    return w, src, jax.ShapeDtypeStruct(shape, jnp.bfloat16), dst, n


def _call(body, grid, in_specs, args, out_struct, out_spec, carry_shapes, scratch,
          emit_state, casts, name):
    in_specs, args = list(in_specs), list(args)
    out_shape, out_specs, scratch = [out_struct], [out_spec], list(scratch)
    n_in = len(args)
    for w, (sblk, sidx), struct, (dblk, didx), n_steps in casts:
        assert grid[0] * grid[1] * grid[2] >= n_steps
        args.append(w)
        in_specs.append(pl.BlockSpec(sblk, functools.partial(sidx, grid=grid)))
        out_shape.append(struct)
        out_specs.append(pl.BlockSpec(dblk, functools.partial(didx, grid=grid)))
    if emit_state:
        for shp in carry_shapes:
            out_shape.append(jax.ShapeDtypeStruct(shp, jnp.float32))
            out_specs.append(_full(shp))
    else:
        scratch += [pltpu.VMEM(shp, jnp.float32) for shp in carry_shapes]
    return pl.pallas_call(
        functools.partial(_entry, body, n_in, tuple(c[4] for c in casts), len(carry_shapes),
                          emit_state),
        grid=grid, in_specs=in_specs, out_specs=out_specs, out_shape=out_shape,
        scratch_shapes=scratch,
        compiler_params=pltpu.CompilerParams(
            dimension_semantics=("arbitrary", "arbitrary", "arbitrary"),
            vmem_limit_bytes=VMEM_LIMIT_BYTES),
        name=name,
    )(*args)


def _full(shape):
    return pl.BlockSpec(shape, lambda bb, i, j, n=len(shape): (0,) * n)


def _mix0_body(tm, pos0, nj, ins, out_ref, carries, scratch):
    (h_ref, gpre_ref, gpost_ref, wgb_ref, wgc_ref, wva_ref, wvb_ref, cw_ref,
     pw_ref, ps_ref, woa_ref, wob_ref, hc_ref, hv_ref) = ins
    carc_ref, carv_ref = carries
    z_ref, ya_ref, pg_ref = scratch[:3]
    staged = (scratch[3:7], scratch[7:11])
    i = pl.program_id(1)
    j = pl.program_id(2)
    _start_tile(i, j, h_ref, gpre_ref, z_ref, out_ref,
                ((hc_ref, carc_ref), (hv_ref, carv_ref)))

    def project(slot):
        z = z_ref[...]
        for dst, w_ref in zip(staged[slot], (wgb_ref, wgc_ref, wva_ref, wvb_ref)):
            dst[...] = _dot(z, w_ref[...])

    def mix(slot, jj):
        gb_ref, gc_ref, va_ref, vb_ref = staged[slot]
        ext_c = _with_history(carc_ref.at[jj], gc_ref[...] * va_ref[...], HIST_CONV3, tm)
        ya = gb_ref[...] * _causal_taps(ext_c, cw_ref, HIST_CONV3, tm, A_CONV)
        ya_ref[...] = ya.astype(jnp.bfloat16)

        vb = vb_ref[...]
        s = _with_history(carv_ref.at[jj], vb, HIST_POOL, tm)
        sums = []
        for step in (1, 2, 4, 8):
            s = s + pltpu.roll(s, step, axis=0)
            sums.append(s[HIST_POOL:HIST_POOL + tm])
        wsum = jnp.where(jj == 0, sums[0],
                         jnp.where(jj == 1, sums[1], jnp.where(jj == 2, sums[2], sums[3])))
        window = jnp.left_shift(2, jj)
        pos = pos0 + i * tm + lax.broadcasted_iota(jnp.int32, (tm, 1), 0)
        count = jnp.minimum(pos + 1, window).astype(jnp.float32)
        pg_ref[...] = (wsum / count - vb).astype(jnp.bfloat16)

    def project_out():
        yb = _dot(pg_ref[...], pw_ref[...]) * ps_ref[...]
        out_ref[...] += (_dot(ya_ref[...], woa_ref[...])
                         + _dot(yb.astype(jnp.bfloat16), wob_ref[...]))

    _pipelined(j, nj, project, mix, project_out,
               lambda: _finish_tile(h_ref, gpost_ref, out_ref))


def _mix0_call(h, gpre, gpost, w_in, conv_w, pool_w, pool_scale, w_out, hist_c, hist_v,
               *, tm, pos0, emit_state, casts=()):
    b, s, d = h.shape
    ck = MIX_CHUNK
    nj = A_WIDTH // ck
    assert B_WINDOWS == tuple(2 << g for g in range(nj)) and B_WIDTH == nj * ck
    row = lambda bb, i, j: (bb, i, 0)
    proj = lambda j: jnp.minimum(j, nj - 1)
    mixed = lambda j: jnp.maximum(j - 1, 0)
    in_specs = [
        pl.BlockSpec((None, tm, d), row),
        _full((1, d)),
        _full((1, d)),
        pl.BlockSpec((d, ck), lambda bb, i, j: (0, proj(j))),
        pl.BlockSpec((d, ck), lambda bb, i, j: (0, nj + proj(j))),
        pl.BlockSpec((d, ck), lambda bb, i, j: (0, 2 * nj + proj(j))),
        pl.BlockSpec((d, ck), lambda bb, i, j: (0, 3 * nj + proj(j))),
        pl.BlockSpec((A_CONV, ck), lambda bb, i, j: (0, mixed(j))),
        pl.BlockSpec((None, ck, ck), lambda bb, i, j: (mixed(j), 0, 0)),
        pl.BlockSpec((1, ck), lambda bb, i, j: (0, mixed(j))),
        pl.BlockSpec((ck, d), lambda bb, i, j: (mixed(j), 0)),
        pl.BlockSpec((ck, d), lambda bb, i, j: (nj + mixed(j), 0)),
        _full((nj, HIST_CONV3, ck)),
        _full((nj, HIST_POOL, ck)),
    ]
    return _call(
        functools.partial(_mix0_body, tm, pos0, nj),
        (b, s // tm, nj + 1), in_specs,
        (h, gpre, gpost, w_in, w_in, w_in, w_in, conv_w, pool_w, pool_scale, w_out, w_out,
         hist_c, hist_v),
        jax.ShapeDtypeStruct((b, s, d), jnp.float32), pl.BlockSpec((None, tm, d), row),
        [(nj, HIST_CONV3, ck), (nj, HIST_POOL, ck)],
        [pltpu.VMEM((tm, d), jnp.bfloat16)] + [pltpu.VMEM((tm, ck), jnp.bfloat16)] * 2
        + [pltpu.VMEM((tm, ck), jnp.float32)] * 8,
        emit_state, casts, f"mix0_tm{tm}")


def _ffn_body(tm, nj, ins, out_ref, carries, scratch):
    (h_ref, gpre_ref, gpost_ref, wg_ref, wv_ref, cwg_ref, cwv_ref, wd_ref,
     hg_ref, hv_ref) = ins
    carg_ref, carv_ref = carries
    z_ref = scratch[0]
    acts = scratch[1:3]
    staged = (scratch[3:5], scratch[5:7])
    i = pl.program_id(1)
    j = pl.program_id(2)
    _start_tile(i, j, h_ref, gpre_ref, z_ref, out_ref,
                ((hg_ref, carg_ref), (hv_ref, carv_ref)))

    def up(slot):
        z = z_ref[...]
        ug_ref, uv_ref = staged[slot]
        ug_ref[...] = _dot(z, wg_ref[...])
        uv_ref[...] = _dot(z, wv_ref[...])

    def gate(slot, jj):
        ug_ref, uv_ref = staged[slot]
        rows = min(GATE_ROWS, tm)
        for c in range(0, ug_ref.shape[1], LANES):
            cols = pl.ds(c, LANES)
            for r in range(0, tm, rows):
                g = _conv3_block(ug_ref, carg_ref.at[jj], cwg_ref, r, rows, cols)
                v = _conv3_block(uv_ref, carv_ref.at[jj], cwv_ref, r, rows, cols)
                acts[slot][pl.ds(r, rows), cols] = (g * _sigmoid(g) * v).astype(jnp.bfloat16)
        carg_ref[jj] = ug_ref[pl.ds(tm - HIST_CONV3, HIST_CONV3), :]
        carv_ref[jj] = uv_ref[pl.ds(tm - HIST_CONV3, HIST_CONV3), :]

    def down(slot):
        out_ref[...] += _dot(acts[slot][...], wd_ref[...])

    _pipelined3(j, nj, up, gate, down, lambda: _finish_tile(h_ref, gpost_ref, out_ref))


def _ffn_call(h, gpre, gpost, w_up, conv_w, w_down, hist_g, hist_v, *, layer, tm, emit_state,
              casts=()):
    b, s, d = h.shape
    dff = w_down.shape[0]
    tf = FFN_CHUNK
    nj = dff // tf
    row = lambda bb, i, j: (bb, i, 0)
    gated = lambda j: jnp.clip(j - 1, 0, nj - 1)
    gcol = lambda bb, i, j: (0, jnp.minimum(j, nj - 1))
    vcol = lambda bb, i, j: (0, nj + jnp.minimum(j, nj - 1))
    gtap = lambda bb, i, j: (layer, 0, gated(j))
    vtap = lambda bb, i, j: (layer, 0, nj + gated(j))
    in_specs = [
        pl.BlockSpec((None, tm, d), row),
        _full((1, d)),
        _full((1, d)),
        pl.BlockSpec((d, tf), gcol),
        pl.BlockSpec((d, tf), vcol),
        pl.BlockSpec((None, FFN_CONV, tf), gtap),
        pl.BlockSpec((None, FFN_CONV, tf), vtap),
        pl.BlockSpec((tf, d), lambda bb, i, j: (jnp.clip(j - 2, 0, nj - 1), 0)),
        _full((nj, HIST_CONV3, tf)),
        _full((nj, HIST_CONV3, tf)),
    ]
    scratch = ([pltpu.VMEM((tm, d), jnp.bfloat16)] + [pltpu.VMEM((tm, tf), jnp.bfloat16)] * 2
               + [pltpu.VMEM((tm, tf), jnp.float32)] * 4)
    return _call(
        functools.partial(_ffn_body, tm, nj),
        (b, s // tm, nj + 2), in_specs,
        (h, gpre, gpost, w_up, w_up, conv_w, conv_w, w_down, hist_g, hist_v),
        jax.ShapeDtypeStruct((b, s, d), jnp.float32), pl.BlockSpec((None, tm, d), row),
        [(nj, HIST_CONV3, tf)] * 2, scratch, emit_state, casts, f"ffn{layer}_tm{tm}")


def _conf_body(tm, n1, ins, out_ref, carries, scratch):
    (h_ref, gpre_ref, gpost_ref, wa_ref, wg_ref, ba_ref, bg_ref, wdw_ref, bdw_ref,
     lng_ref, lnb_ref, w2_ref, b2_ref, hist_ref) = ins
    (car_ref,) = carries
    z_ref, cbuf_ref, mean_ref, rstd_ref, sh_ref = scratch[:5]
    staged = (scratch[5:7], scratch[7:9])
    i = pl.program_id(1)
    j = pl.program_id(2)
    cw = CONF_CHUNK
    _start_tile(i, j, h_ref, gpre_ref, z_ref, None, ((hist_ref, car_ref),))

    def project(slot):
        z = z_ref[...]
        a_ref, g_ref = staged[slot]
        a_ref[...] = _dot(z, wa_ref[...])
        g_ref[...] = _dot(z, wg_ref[...])

    def conv(slot, jj):
        a_ref, g_ref = staged[slot]
        glu = (a_ref[...] + ba_ref[...]) * _sigmoid(g_ref[...] + bg_ref[...])
        ext = _with_history(car_ref.at[jj], glu, HIST_CONV31, tm)
        _causal_taps_blocked(ext, sh_ref, wdw_ref, bdw_ref, cbuf_ref.at[jj],
                             HIST_CONV31, tm, C_CONV)

    def stats():
        total = None
        for q in range(n1):
            part = jnp.sum(cbuf_ref[q], axis=-1, keepdims=True)
            total = part if total is None else total + part
        mean = total / (n1 * cw)
        total = None
        for q in range(n1):
            dev = cbuf_ref[q] - mean
            part = jnp.sum(dev * dev, axis=-1, keepdims=True)
            total = part if total is None else total + part
        mean_ref[...] = mean
        rstd_ref[...] = lax.rsqrt(total / (n1 * cw) + EPS)
        out_ref[...] = jnp.broadcast_to(b2_ref[...], out_ref.shape)

    _pipelined(j, n1, project, conv, lambda: None, stats)

    @pl.when(j > n1)
    def _():
        y = ((cbuf_ref[j - n1 - 1] - mean_ref[...]) * rstd_ref[...] * lng_ref[...]
             + lnb_ref[...])
        s = (y * _sigmoid(y)).astype(jnp.bfloat16)
        out_ref[...] += _dot(s, w2_ref[...])

    @pl.when(j == 2 * n1)
    def _():
        _finish_tile(h_ref, gpost_ref, out_ref)


def _conf_call(h, gpre, gpost, w1, b1, wdw, bdw, lng, lnb, w2, b2, hist,
               *, tm, emit_state, casts=()):
    b, s, d = h.shape
    c = w2.shape[0]
    cw = CONF_CHUNK
    n1 = c // cw
    row = lambda bb, i, j: (bb, i, 0)
    acol = lambda bb, i, j: (0, jnp.minimum(j, n1 - 1))
    gcol = lambda bb, i, j: (0, n1 + jnp.minimum(j, n1 - 1))
    ccol = lambda bb, i, j: (0, jnp.clip(j - 1, 0, n1 - 1))
    cgcol = lambda bb, i, j: (0, n1 + jnp.clip(j - 1, 0, n1 - 1))
    kcol = lambda bb, i, j: (0, jnp.maximum(j - n1 - 1, 0))
    in_specs = [
        pl.BlockSpec((None, tm, d), row),
        _full((1, d)),
        _full((1, d)),
        pl.BlockSpec((d, cw), acol),
        pl.BlockSpec((d, cw), gcol),
        pl.BlockSpec((1, cw), ccol),
        pl.BlockSpec((1, cw), cgcol),
        pl.BlockSpec((C_CONV, cw), ccol),
        pl.BlockSpec((1, cw), ccol),
        pl.BlockSpec((1, cw), kcol),
        pl.BlockSpec((1, cw), kcol),
        pl.BlockSpec((cw, d), lambda bb, i, j: (jnp.maximum(j - n1 - 1, 0), 0)),
        _full((1, d)),
        _full((n1, HIST_CONV31, cw)),
    ]
    scratch = [
        pltpu.VMEM((tm, d), jnp.bfloat16),
        pltpu.VMEM((n1, tm, cw), jnp.float32),
        pltpu.VMEM((tm, 1), jnp.float32),
        pltpu.VMEM((tm, 1), jnp.float32),
        pltpu.VMEM((SUBLANES, HIST_CONV31 + tm, cw), jnp.float32),
    ] + [pltpu.VMEM((tm, cw), jnp.float32)] * 4
    return _call(
        functools.partial(_conf_body, tm, n1),
        (b, s // tm, 2 * n1 + 1), in_specs,
        (h, gpre, gpost, w1, w1, b1, b1, wdw, bdw, lng, lnb, w2, b2, hist),
        jax.ShapeDtypeStruct((b, s, d), jnp.float32), pl.BlockSpec((None, tm, d), row),
        [(n1, HIST_CONV31, cw)], scratch, emit_state, casts, f"conf_tm{tm}")


def kernel(x, meta_tokens, mix_pre_g, mix_post_g, ffn_pre_g, ffn_post_g, ab_w_in, ab_conv_w, ab_pool_w, ab_pool_scale, ab_w_out, c_w_pw1, c_b_pw1, c_w_dw, c_b_dw, c_ln_g, c_ln_b, c_w_pw2, c_b_pw2, ffn_w_up, ffn_conv_w, ffn_w_down):
    depth, d = mix_pre_g.shape
    dff = ffn_w_down.shape[1]
    bf = jnp.bfloat16
    f32 = jnp.float32
    assert depth == 2 and x.shape[1] % TM_MAIN == 0

    def zeros(hp, width, chunk):
        return jnp.zeros((width // chunk, hp, chunk), f32)

    def ffn(hm, hx, layer, w_up, w_down, casts):
        fargs = (ffn_pre_g[layer][None], ffn_post_g[layer][None], w_up, ffn_conv_w, w_down)
        zero_u = zeros(HIST_CONV3, dff, FFN_CHUNK)
        hm, st_g, st_v = _ffn_call(hm, *fargs, zero_u, zero_u,
                                   layer=layer, tm=N_META, emit_state=True)
        hx, *cast = _ffn_call(hx, *fargs, st_g, st_v, layer=layer, tm=TM_MAIN,
                              emit_state=False, casts=casts)
        return hm, hx, cast

    hm = meta_tokens.astype(x.dtype)[None]

    args = (mix_pre_g[0][None], mix_post_g[0][None], ab_w_in[0].astype(bf), ab_conv_w[0],
            ab_pool_w[0].astype(bf), ab_pool_scale[0][None], ab_w_out[0].astype(bf))
    hm, st_c, st_v = _mix0_call(
        hm, *args, zeros(HIST_CONV3, A_WIDTH, MIX_CHUNK), zeros(HIST_POOL, B_WIDTH, MIX_CHUNK),
        tm=N_META, pos0=0, emit_state=True)
    hx, w_up0, w_down0 = _mix0_call(
        x, *args, st_c, st_v, tm=TM_MAIN, pos0=N_META, emit_state=False,
        casts=(_cast_job(ffn_w_up, 0, 32), _cast_job(ffn_w_down, 0, 128)))

    hm, hx, (w1, w2, w_up1, w_down1) = ffn(
        hm, hx, 0, w_up0, w_down0,
        (_cast_job(c_w_pw1, 0, 16), _cast_job(c_w_pw2, 0, 16),
         _cast_job(ffn_w_up, 1, 16), _cast_job(ffn_w_down, 1, 32)))

    args = (mix_pre_g[1][None], mix_post_g[1][None], w1, c_b_pw1[0][None], c_w_dw[0],
            c_b_dw[0][None], c_ln_g[0][None], c_ln_b[0][None], w2, c_b_pw2[0][None])
    hm, st = _conf_call(hm, *args, zeros(HIST_CONV31, c_w_pw2.shape[1], CONF_CHUNK),
                        tm=N_META, emit_state=True)
    (hx,) = _conf_call(hx, *args, st, tm=TM_MAIN, emit_state=False)

    _, hx, _ = ffn(hm, hx, 1, w_up1, w_down1, ())
    return hx
```

```python
import functools

import jax
import jax.numpy as jnp
from jax import lax
from jax.experimental import pallas as pl
from jax.experimental.pallas import tpu as pltpu

EPS = 1e-6
N_META = 16
A_WIDTH = 1024
B_WIDTH = 1024
B_WINDOWS = (2, 4, 8, 16)
C_CONV = 31
A_CONV = 3
FFN_CONV = 3

SUBLANES = 8
HIST_CONV3 = 8
HIST_POOL = 16
HIST_CONV31 = 32

TM_MAIN = 512
MIX_CHUNK = 256
FFN_CHUNK = 512
CONF_CHUNK = 512

VMEM_LIMIT_BYTES = 56 * 1024 * 1024


def _rms(x, g):
    ms = jnp.mean(x * x, axis=-1, keepdims=True)
    return x * lax.rsqrt(ms + EPS) * g


def _sigmoid(x):
    return 0.5 * jnp.tanh(0.5 * x) + 0.5


def _dot(a, b):
    return jnp.dot(a, b, preferred_element_type=jnp.float32)


def _with_history(carry_ref, cur, hp, tm):
    ext = jnp.concatenate([carry_ref[...], cur], axis=0)
    carry_ref[...] = ext[tm:tm + hp]
    return ext


def _causal_taps(ext, w_ref, hp, tm, k):
    delayed = [ext] + [pltpu.roll(ext, b, axis=0) for b in range(1, min(k, SUBLANES))]
    acc = None
    for q in range(k):
        a, b = divmod(k - 1 - q, SUBLANES)
        lo = hp - SUBLANES * a
        term = w_ref[pl.ds(q, 1), :] * delayed[b][lo:lo + tm]
        acc = term if acc is None else acc + term
    return acc


def _start_tile(i, j, h_ref, gpre_ref, z_ref, out_ref, carries):
    @pl.when(j == 0)
    def _():
        z_ref[...] = _rms(h_ref[...], gpre_ref[...]).astype(jnp.bfloat16)
        if out_ref is not None:
            out_ref[...] = jnp.zeros_like(out_ref)

    @pl.when(jnp.logical_and(i == 0, j == 0))
    def _():
        for hist_ref, carry_ref in carries:
            carry_ref[...] = hist_ref[...]


def _finish_tile(h_ref, gpost_ref, out_ref):
    out_ref[...] = h_ref[...] + _rms(out_ref[...], gpost_ref[...])


def _flat_step():
    return ((pl.program_id(0) * pl.num_programs(1) + pl.program_id(1)) * pl.num_programs(2)
            + pl.program_id(2))


def _entry(body, n_in, cast_steps, n_carry, emit_state, *refs):
    n_cast = len(cast_steps)
    ins = refs[:n_in]
    cast_src = refs[n_in:n_in + n_cast]
    o = n_in + n_cast
    out_ref = refs[o]
    cast_dst = refs[o + 1:o + 1 + n_cast]
    o += 1 + n_cast
    if emit_state:
        carries, scratch = refs[o:o + n_carry], refs[o + n_carry:]
    else:
        scratch, carries = refs[o:len(refs) - n_carry], refs[len(refs) - n_carry:]
    step = _flat_step()
    for src, dst, n in zip(cast_src, cast_dst, cast_steps):
        @pl.when(step < n)
        def _(src=src, dst=dst):
            dst[...] = src[...].astype(jnp.bfloat16)
    body(ins, out_ref, carries, scratch)


def _cast_job(w, lead, rows):
    shape = w.shape if lead is None else w.shape[1:]
    n = shape[0] // rows
    assert n * rows == shape[0]

    def make(prefix, block):
        def index(bb, i, j, grid):
            step = (bb * grid[1] + i) * grid[2] + j
            return prefix + (jnp.minimum(step, n - 1), 0)
        return block, index

    src = make(() if lead is None else (lead,), ((rows, shape[1]) if lead is None
                                                 else (None, rows, shape[1])))
    dst = make((), (rows, shape[1]))
    return w, src, jax.ShapeDtypeStruct(shape, jnp.bfloat16), dst, n


def _call(body, grid, in_specs, args, out_struct, out_spec, carry_shapes, scratch,
          emit_state, casts, name):
    in_specs, args = list(in_specs), list(args)
    out_shape, out_specs, scratch = [out_struct], [out_spec], list(scratch)
    n_in = len(args)
    for w, (sblk, sidx), struct, (dblk, didx), n_steps in casts:
        assert grid[0] * grid[1] * grid[2] >= n_steps
        args.append(w)
        in_specs.append(pl.BlockSpec(sblk, functools.partial(sidx, grid=grid)))
        out_shape.append(struct)
        out_specs.append(pl.BlockSpec(dblk, functools.partial(didx, grid=grid)))
    if emit_state:
        for shp in carry_shapes:
            out_shape.append(jax.ShapeDtypeStruct(shp, jnp.float32))
            out_specs.append(_full(shp))
    else:
        scratch += [pltpu.VMEM(shp, jnp.float32) for shp in carry_shapes]
    return pl.pallas_call(
        functools.partial(_entry, body, n_in, tuple(c[4] for c in casts), len(carry_shapes),
                          emit_state),
        grid=grid, in_specs=in_specs, out_specs=out_specs, out_shape=out_shape,
        scratch_shapes=scratch,
        compiler_params=pltpu.CompilerParams(
            dimension_semantics=("arbitrary", "arbitrary", "arbitrary"),
            vmem_limit_bytes=VMEM_LIMIT_BYTES),
        name=name,
    )(*args)


def _full(shape):
    return pl.BlockSpec(shape, lambda bb, i, j, n=len(shape): (0,) * n)


def _mix0_body(tm, pos0, ins, out_ref, carries, scratch):
    (h_ref, gpre_ref, gpost_ref, wgb_ref, wgc_ref, wva_ref, wvb_ref, cw_ref,
     pw_ref, ps_ref, woa_ref, wob_ref, hc_ref, hv_ref) = ins
    carc_ref, carv_ref = carries
    (z_ref,) = scratch
    i = pl.program_id(1)
    j = pl.program_id(2)
    nj = pl.num_programs(2)
    _start_tile(i, j, h_ref, gpre_ref, z_ref, out_ref,
                ((hc_ref, carc_ref), (hv_ref, carv_ref)))
    z = z_ref[...]

    c = _dot(z, wgc_ref[...]) * _dot(z, wva_ref[...])
    ext_c = _with_history(carc_ref.at[j], c, HIST_CONV3, tm)
    ya = _dot(z, wgb_ref[...]) * _causal_taps(ext_c, cw_ref, HIST_CONV3, tm, A_CONV)

    vb = _dot(z, wvb_ref[...])
    s = _with_history(carv_ref.at[j], vb, HIST_POOL, tm)
    sums = []
    for step in (1, 2, 4, 8):
        s = s + pltpu.roll(s, step, axis=0)
        sums.append(s[HIST_POOL:HIST_POOL + tm])
    wsum = jnp.where(j == 0, sums[0],
                     jnp.where(j == 1, sums[1], jnp.where(j == 2, sums[2], sums[3])))
    window = jnp.left_shift(2, j)
    pos = pos0 + i * tm + lax.broadcasted_iota(jnp.int32, (tm, 1), 0)
    count = jnp.minimum(pos + 1, window).astype(jnp.float32)
    pg = wsum / count - vb
    yb = _dot(pg.astype(jnp.bfloat16), pw_ref[...]) * ps_ref[...]

    out_ref[...] += (_dot(ya.astype(jnp.bfloat16), woa_ref[...])
                     + _dot(yb.astype(jnp.bfloat16), wob_ref[...]))

    @pl.when(j == nj - 1)
    def _():
        _finish_tile(h_ref, gpost_ref, out_ref)


def _mix0_call(h, gpre, gpost, w_in, conv_w, pool_w, pool_scale, w_out, hist_c, hist_v,
               *, tm, pos0, emit_state, casts=()):
    b, s, d = h.shape
    ck = MIX_CHUNK
    nj = A_WIDTH // ck
    assert B_WINDOWS == tuple(2 << g for g in range(nj)) and B_WIDTH == nj * ck
    row = lambda bb, i, j: (bb, i, 0)
    in_specs = [
        pl.BlockSpec((None, tm, d), row),
        _full((1, d)),
        _full((1, d)),
        pl.BlockSpec((d, ck), lambda bb, i, j: (0, j)),
        pl.BlockSpec((d, ck), lambda bb, i, j: (0, nj + j)),
        pl.BlockSpec((d, ck), lambda bb, i, j: (0, 2 * nj + j)),
        pl.BlockSpec((d, ck), lambda bb, i, j: (0, 3 * nj + j)),
        pl.BlockSpec((A_CONV, ck), lambda bb, i, j: (0, j)),
        pl.BlockSpec((None, ck, ck), lambda bb, i, j: (j, 0, 0)),
        pl.BlockSpec((1, ck), lambda bb, i, j: (0, j)),
        pl.BlockSpec((ck, d), lambda bb, i, j: (j, 0)),
        pl.BlockSpec((ck, d), lambda bb, i, j: (nj + j, 0)),
        _full((nj, HIST_CONV3, ck)),
        _full((nj, HIST_POOL, ck)),
    ]
    return _call(
        functools.partial(_mix0_body, tm, pos0),
        (b, s // tm, nj), in_specs,
        (h, gpre, gpost, w_in, w_in, w_in, w_in, conv_w, pool_w, pool_scale, w_out, w_out,
         hist_c, hist_v),
        jax.ShapeDtypeStruct((b, s, d), jnp.float32), pl.BlockSpec((None, tm, d), row),
        [(nj, HIST_CONV3, ck), (nj, HIST_POOL, ck)],
        [pltpu.VMEM((tm, d), jnp.bfloat16)], emit_state, casts, f"mix0_tm{tm}")


def _ffn_body(tm, ins, out_ref, carries, scratch):
    (h_ref, gpre_ref, gpost_ref, wg_ref, wv_ref, cwg_ref, cwv_ref, wd_ref,
     hg_ref, hv_ref) = ins
    carg_ref, carv_ref = carries
    (z_ref,) = scratch
    i = pl.program_id(1)
    j = pl.program_id(2)
    nj = pl.num_programs(2)
    _start_tile(i, j, h_ref, gpre_ref, z_ref, out_ref,
                ((hg_ref, carg_ref), (hv_ref, carv_ref)))
    z = z_ref[...]
    ext_g = _with_history(carg_ref.at[j], _dot(z, wg_ref[...]), HIST_CONV3, tm)
    ext_v = _with_history(carv_ref.at[j], _dot(z, wv_ref[...]), HIST_CONV3, tm)
    g = _causal_taps(ext_g, cwg_ref, HIST_CONV3, tm, FFN_CONV)
    v = _causal_taps(ext_v, cwv_ref, HIST_CONV3, tm, FFN_CONV)
    act = (g * _sigmoid(g) * v).astype(jnp.bfloat16)
    out_ref[...] += _dot(act, wd_ref[...])

    @pl.when(j == nj - 1)
    def _():
        _finish_tile(h_ref, gpost_ref, out_ref)


def _ffn_call(h, gpre, gpost, w_up, conv_w, w_down, hist_g, hist_v, *, layer, tm, emit_state,
              casts=()):
    b, s, d = h.shape
    dff = w_down.shape[0]
    tf = FFN_CHUNK
    nj = dff // tf
    row = lambda bb, i, j: (bb, i, 0)
    in_specs = [
        pl.BlockSpec((None, tm, d), row),
        _full((1, d)),
        _full((1, d)),
        pl.BlockSpec((d, tf), lambda bb, i, j: (0, j)),
        pl.BlockSpec((d, tf), lambda bb, i, j: (0, nj + j)),
        pl.BlockSpec((None, FFN_CONV, tf), lambda bb, i, j: (layer, 0, j)),
        pl.BlockSpec((None, FFN_CONV, tf), lambda bb, i, j: (layer, 0, nj + j)),
        pl.BlockSpec((tf, d), lambda bb, i, j: (j, 0)),
        _full((nj, HIST_CONV3, tf)),
        _full((nj, HIST_CONV3, tf)),
    ]
    return _call(
        functools.partial(_ffn_body, tm),
        (b, s // tm, nj), in_specs,
        (h, gpre, gpost, w_up, w_up, conv_w, conv_w, w_down, hist_g, hist_v),
        jax.ShapeDtypeStruct((b, s, d), jnp.float32), pl.BlockSpec((None, tm, d), row),
        [(nj, HIST_CONV3, tf)] * 2, [pltpu.VMEM((tm, d), jnp.bfloat16)], emit_state, casts,
        f"ffn{layer}_tm{tm}")


def _conf_body(tm, n1, ins, out_ref, carries, scratch):
    (h_ref, gpre_ref, gpost_ref, wa_ref, wg_ref, ba_ref, bg_ref, wdw_ref, bdw_ref,
     lng_ref, lnb_ref, w2_ref, b2_ref, hist_ref) = ins
    (car_ref,) = carries
    z_ref, cbuf_ref, mean_ref, rstd_ref = scratch
    i = pl.program_id(1)
    j = pl.program_id(2)
    nj = pl.num_programs(2)
    cw = CONF_CHUNK
    _start_tile(i, j, h_ref, gpre_ref, z_ref, None, ((hist_ref, car_ref),))

    @pl.when(j < n1)
    def _():
        z = z_ref[...]
        a = _dot(z, wa_ref[...]) + ba_ref[...]
        g = _dot(z, wg_ref[...]) + bg_ref[...]
        ext = _with_history(car_ref.at[j], a * _sigmoid(g), HIST_CONV31, tm)
        cbuf_ref[j] = _causal_taps(ext, wdw_ref, HIST_CONV31, tm, C_CONV) + bdw_ref[...]

    @pl.when(j == n1)
    def _():
        total = None
        for q in range(n1):
            part = jnp.sum(cbuf_ref[q], axis=-1, keepdims=True)
            total = part if total is None else total + part
        mean = total / (n1 * cw)
        total = None
        for q in range(n1):
            dev = cbuf_ref[q] - mean
            part = jnp.sum(dev * dev, axis=-1, keepdims=True)
            total = part if total is None else total + part
        mean_ref[...] = mean
        rstd_ref[...] = lax.rsqrt(total / (n1 * cw) + EPS)
        out_ref[...] = jnp.broadcast_to(b2_ref[...], out_ref.shape)

    @pl.when(j >= n1)
    def _():
        y = (cbuf_ref[j - n1] - mean_ref[...]) * rstd_ref[...] * lng_ref[...] + lnb_ref[...]
        s = (y * _sigmoid(y)).astype(jnp.bfloat16)
        out_ref[...] += _dot(s, w2_ref[...])

    @pl.when(j == nj - 1)
    def _():
        _finish_tile(h_ref, gpost_ref, out_ref)


def _conf_call(h, gpre, gpost, w1, b1, wdw, bdw, lng, lnb, w2, b2, hist,
               *, tm, emit_state, casts=()):
    b, s, d = h.shape
    c = w2.shape[0]
    cw = CONF_CHUNK
    n1 = c // cw
    row = lambda bb, i, j: (bb, i, 0)
    acol = lambda bb, i, j: (0, jnp.minimum(j, n1 - 1))
    gcol = lambda bb, i, j: (0, n1 + jnp.minimum(j, n1 - 1))
    kcol = lambda bb, i, j: (0, jnp.maximum(j - n1, 0))
    in_specs = [
        pl.BlockSpec((None, tm, d), row),
        _full((1, d)),
        _full((1, d)),
        pl.BlockSpec((d, cw), acol),
        pl.BlockSpec((d, cw), gcol),
        pl.BlockSpec((1, cw), acol),
        pl.BlockSpec((1, cw), gcol),
        pl.BlockSpec((C_CONV, cw), acol),
        pl.BlockSpec((1, cw), acol),
        pl.BlockSpec((1, cw), kcol),
        pl.BlockSpec((1, cw), kcol),
        pl.BlockSpec((cw, d), lambda bb, i, j: (jnp.maximum(j - n1, 0), 0)),
        _full((1, d)),
        _full((n1, HIST_CONV31, cw)),
    ]
    scratch = [
        pltpu.VMEM((tm, d), jnp.bfloat16),
        pltpu.VMEM((n1, tm, cw), jnp.float32),
        pltpu.VMEM((tm, 1), jnp.float32),
        pltpu.VMEM((tm, 1), jnp.float32),
    ]
    return _call(
        functools.partial(_conf_body, tm, n1),
        (b, s // tm, 2 * n1), in_specs,
        (h, gpre, gpost, w1, w1, b1, b1, wdw, bdw, lng, lnb, w2, b2, hist),
        jax.ShapeDtypeStruct((b, s, d), jnp.float32), pl.BlockSpec((None, tm, d), row),
        [(n1, HIST_CONV31, cw)], scratch, emit_state, casts, f"conf_tm{tm}")


def kernel(x, meta_tokens, mix_pre_g, mix_post_g, ffn_pre_g, ffn_post_g, ab_w_in, ab_conv_w, ab_pool_w, ab_pool_scale, ab_w_out, c_w_pw1, c_b_pw1, c_w_dw, c_b_dw, c_ln_g, c_ln_b, c_w_pw2, c_b_pw2, ffn_w_up, ffn_conv_w, ffn_w_down):
    depth, d = mix_pre_g.shape
    dff = ffn_w_down.shape[1]
    bf = jnp.bfloat16
    f32 = jnp.float32
    assert depth == 2 and x.shape[1] % TM_MAIN == 0

    def zeros(hp, width, chunk):
        return jnp.zeros((width // chunk, hp, chunk), f32)

    def ffn(hm, hx, layer, w_up, w_down, casts):
        fargs = (ffn_pre_g[layer][None], ffn_post_g[layer][None], w_up, ffn_conv_w, w_down)
        zero_u = zeros(HIST_CONV3, dff, FFN_CHUNK)
        hm, st_g, st_v = _ffn_call(hm, *fargs, zero_u, zero_u,
                                   layer=layer, tm=N_META, emit_state=True)
        hx, *cast = _ffn_call(hx, *fargs, st_g, st_v, layer=layer, tm=TM_MAIN,
                              emit_state=False, casts=casts)
        return hm, hx, cast

    hm = meta_tokens.astype(x.dtype)[None]

    args = (mix_pre_g[0][None], mix_post_g[0][None], ab_w_in[0].astype(bf), ab_conv_w[0],
            ab_pool_w[0].astype(bf), ab_pool_scale[0][None], ab_w_out[0].astype(bf))
    hm, st_c, st_v = _mix0_call(
        hm, *args, zeros(HIST_CONV3, A_WIDTH, MIX_CHUNK), zeros(HIST_POOL, B_WIDTH, MIX_CHUNK),
        tm=N_META, pos0=0, emit_state=True)
    hx, w_up0, w_down0 = _mix0_call(
        x, *args, st_c, st_v, tm=TM_MAIN, pos0=N_META, emit_state=False,
        casts=(_cast_job(ffn_w_up, 0, 32), _cast_job(ffn_w_down, 0, 128)))

    hm, hx, (w1, w2, w_up1, w_down1) = ffn(
        hm, hx, 0, w_up0, w_down0,
        (_cast_job(c_w_pw1, 0, 16), _cast_job(c_w_pw2, 0, 16),
         _cast_job(ffn_w_up, 1, 16), _cast_job(ffn_w_down, 1, 32)))

    args = (mix_pre_g[1][None], mix_post_g[1][None], w1, c_b_pw1[0][None], c_w_dw[0],
            c_b_dw[0][None], c_ln_g[0][None], c_ln_b[0][None], w2, c_b_pw2[0][None])
    hm, st = _conf_call(hm, *args, zeros(HIST_CONV31, c_w_pw2.shape[1], CONF_CHUNK),
                        tm=N_META, emit_state=True)
    (hx,) = _conf_call(hx, *args, st, tm=TM_MAIN, emit_state=False)

    _, hx, _ = ffn(hm, hx, 1, w_up1, w_down1, ())
    return hx
```

```python
import functools

import jax
import jax.numpy as jnp
from jax import lax
from jax.experimental import pallas as pl
from jax.experimental.pallas import tpu as pltpu

EPS = 1e-6
N_META = 16
A_WIDTH = 1024
B_WIDTH = 1024
B_WINDOWS = (2, 4, 8, 16)
C_CONV = 31
A_CONV = 3
FFN_CONV = 3

SUBLANES = 8
LANES = 128
TAP_ROWS = 32
GATE_ROWS = 64
NORM_ROWS = 16
HIST_CONV3 = 8
HIST_POOL = 16
HIST_CONV31 = 32

TM_MAIN = 512
MIX_CHUNK = 256
FFN_CHUNK = 512
CONF_CHUNK = 512
CONF_PASS = 256

VMEM_LIMIT_BYTES = 56 * 1024 * 1024


def _rms(x, g):
    ms = jnp.mean(x * x, axis=-1, keepdims=True)
    return x * lax.rsqrt(ms + EPS) * g


def _sigmoid(x):
    return 0.5 * jnp.tanh(0.5 * x) + 0.5


def _dot(a, b):
    return jnp.dot(a, b, preferred_element_type=jnp.float32)


def _with_history(carry_ref, cur, hp, tm):
    ext = jnp.concatenate([carry_ref[...], cur], axis=0)
    carry_ref[...] = ext[tm:tm + hp]
    return ext


def _causal_taps(ext, w_ref, hp, tm, k):
    delayed = [ext] + [pltpu.roll(ext, b, axis=0) for b in range(1, min(k, SUBLANES))]
    acc = None
    for q in range(k):
        a, b = divmod(k - 1 - q, SUBLANES)
        lo = hp - SUBLANES * a
        term = w_ref[pl.ds(q, 1), :] * delayed[b][lo:lo + tm]
        acc = term if acc is None else acc + term
    return acc


def _stage_slabs(ext_ref, carry_ref, cur, hp, tm, first=0):
    for k in range(cur.shape[1] // LANES):
        c = first + k
        cols = pl.ds(c * LANES, LANES)
        ext_ref[c, pl.ds(0, hp), :] = carry_ref[:, cols]
        ext_ref[c, pl.ds(hp, tm), :] = cur[:, k * LANES:(k + 1) * LANES]
        carry_ref[:, cols] = ext_ref[c, pl.ds(tm, hp), :]


def _taps_from_slabs(ext_ref, w_ref, b_ref, dst_ref, hp, tm, k, slabs=None):
    rows = min(TAP_ROWS, tm)
    for c in (range(ext_ref.shape[0]) if slabs is None else slabs):
        cols = pl.ds(c * LANES, LANES)
        for r in range(0, tm, rows):
            dst_ref[pl.ds(r, rows), cols] = (b_ref[:, cols]
                                             + _taps_block(ext_ref, c, w_ref, hp, r, rows, k))


def _taps_block(ext_ref, c, w_ref, hp, r, rows, k):
    cols = pl.ds(c * LANES, LANES)
    acc = None
    for q in range(k):
        window = ext_ref[c, pl.ds(hp + r - (k - 1 - q), rows, stride=1), :]
        term = w_ref[pl.ds(q, 1), cols] * window
        acc = term if acc is None else acc + term
    return acc


def _row_blocks(ref):
    rows = min(NORM_ROWS, ref.shape[0])
    return [pl.ds(r, rows) for r in range(0, ref.shape[0], rows)]


def _start_tile(i, j, h_ref, gpre_ref, z_ref, out_ref, carries):
    @pl.when(j == 0)
    def _():
        z_ref[...] = _rms(h_ref[...], gpre_ref[...]).astype(jnp.bfloat16)
        if out_ref is not None:
            out_ref[...] = jnp.zeros_like(out_ref)

    @pl.when(jnp.logical_and(i == 0, j == 0))
    def _():
        for hist_ref, carry_ref in carries:
            carry_ref[...] = hist_ref[...]


def _finish_tile(h_ref, gpost_ref, out_ref):
    for rows in _row_blocks(out_ref):
        out_ref[rows, :] = h_ref[rows, :] + _rms(out_ref[rows, :], gpost_ref[...])


def _flat_step():
    return ((pl.program_id(0) * pl.num_programs(1) + pl.program_id(1)) * pl.num_programs(2)
            + pl.program_id(2))


def _entry(body, n_in, cast_steps, n_carry, emit_state, *refs):
    n_cast = len(cast_steps)
    ins = refs[:n_in]
    cast_src = refs[n_in:n_in + n_cast]
    o = n_in + n_cast
    out_ref = refs[o]
    cast_dst = refs[o + 1:o + 1 + n_cast]
    o += 1 + n_cast
    if emit_state:
        carries, scratch = refs[o:o + n_carry], refs[o + n_carry:]
    else:
        scratch, carries = refs[o:len(refs) - n_carry], refs[len(refs) - n_carry:]
    step = _flat_step()
    for src, dst, n in zip(cast_src, cast_dst, cast_steps):
        @pl.when(step < n)
        def _(src=src, dst=dst):
            dst[...] = src[...].astype(jnp.bfloat16)
    body(ins, out_ref, carries, scratch)


def _cast_job(w, lead, rows):
    shape = w.shape if lead is None else w.shape[1:]
    n = shape[0] // rows
    assert n * rows == shape[0]

    def make(prefix, block):
        def index(bb, i, j, grid):
            step = (bb * grid[1] + i) * grid[2] + j
            return prefix + (jnp.minimum(step, n - 1), 0)
        return block, index

    src = make(() if lead is None else (lead,), ((rows, shape[1]) if lead is None
                                                 else (None, rows, shape[1])))
    dst = make((), (rows, shape[1]))
    return w, src, jax.ShapeDtypeStruct(shape, jnp.bfloat16), dst, n


def _call(body, grid, in_specs, args, out_struct, out_spec, carry_shapes, scratch,
          emit_state, casts, name):
    in_specs, args = list(in_specs), list(args)
    out_shape, out_specs, scratch = [out_struct], [out_spec], list(scratch)
    n_in = len(args)
    for w, (sblk, sidx), struct, (dblk, didx), n_steps in casts:
        assert grid[0] * grid[1] * grid[2] >= n_steps
        args.append(w)
        in_specs.append(pl.BlockSpec(sblk, functools.partial(sidx, grid=grid)))
        out_shape.append(struct)
        out_specs.append(pl.BlockSpec(dblk, functools.partial(didx, grid=grid)))
    if emit_state:
        for shp in carry_shapes:
            out_shape.append(jax.ShapeDtypeStruct(shp, jnp.float32))
            out_specs.append(_full(shp))
    else:
        scratch += [pltpu.VMEM(shp, jnp.float32) for shp in carry_shapes]
    return pl.pallas_call(
        functools.partial(_entry, body, n_in, tuple(c[4] for c in casts), len(carry_shapes),
                          emit_state),
        grid=grid, in_specs=in_specs, out_specs=out_specs, out_shape=out_shape,
        scratch_shapes=scratch,
        compiler_params=pltpu.CompilerParams(
            dimension_semantics=("arbitrary", "arbitrary", "arbitrary"),
            vmem_limit_bytes=VMEM_LIMIT_BYTES),
        name=name,
    )(*args)


def _full(shape):
    return pl.BlockSpec(shape, lambda bb, i, j, n=len(shape): (0,) * n)


def _mix0_body(tm, pos0, ins, out_ref, carries, scratch):
    (h_ref, gpre_ref, gpost_ref, wgb_ref, wgc_ref, wva_ref, wvb_ref, cw_ref,
     pw_ref, ps_ref, woa_ref, wob_ref, hc_ref, hv_ref) = ins
    carc_ref, carv_ref = carries
    (z_ref,) = scratch
    i = pl.program_id(1)
    j = pl.program_id(2)
    nj = pl.num_programs(2)
    _start_tile(i, j, h_ref, gpre_ref, z_ref, out_ref,
                ((hc_ref, carc_ref), (hv_ref, carv_ref)))
    z = z_ref[...]

    c = _dot(z, wgc_ref[...]) * _dot(z, wva_ref[...])
    ext_c = _with_history(carc_ref.at[j], c, HIST_CONV3, tm)
    ya = _dot(z, wgb_ref[...]) * _causal_taps(ext_c, cw_ref, HIST_CONV3, tm, A_CONV)

    vb = _dot(z, wvb_ref[...])
    s = _with_history(carv_ref.at[j], vb, HIST_POOL, tm)
    sums = []
    for step in (1, 2, 4, 8):
        s = s + pltpu.roll(s, step, axis=0)
        sums.append(s[HIST_POOL:HIST_POOL + tm])
    wsum = jnp.where(j == 0, sums[0],
                     jnp.where(j == 1, sums[1], jnp.where(j == 2, sums[2], sums[3])))
    window = jnp.left_shift(2, j)
    pos = pos0 + i * tm + lax.broadcasted_iota(jnp.int32, (tm, 1), 0)
    count = jnp.minimum(pos + 1, window).astype(jnp.float32)
    pg = wsum / count - vb
    yb = _dot(pg.astype(jnp.bfloat16), pw_ref[...]) * ps_ref[...]

    out_ref[...] += (_dot(ya.astype(jnp.bfloat16), woa_ref[...])
                     + _dot(yb.astype(jnp.bfloat16), wob_ref[...]))

    @pl.when(j == nj - 1)
    def _():
        _finish_tile(h_ref, gpost_ref, out_ref)


def _mix0_call(h, gpre, gpost, w_in, conv_w, pool_w, pool_scale, w_out, hist_c, hist_v,
               *, tm, pos0, emit_state, casts=()):
    b, s, d = h.shape
    ck = MIX_CHUNK
    nj = A_WIDTH // ck
    assert B_WINDOWS == tuple(2 << g for g in range(nj)) and B_WIDTH == nj * ck
    row = lambda bb, i, j: (bb, i, 0)
    in_specs = [
        pl.BlockSpec((None, tm, d), row),
        _full((1, d)),
        _full((1, d)),
        pl.BlockSpec((d, ck), lambda bb, i, j: (0, j)),
        pl.BlockSpec((d, ck), lambda bb, i, j: (0, nj + j)),
        pl.BlockSpec((d, ck), lambda bb, i, j: (0, 2 * nj + j)),
        pl.BlockSpec((d, ck), lambda bb, i, j: (0, 3 * nj + j)),
        pl.BlockSpec((A_CONV, ck), lambda bb, i, j: (0, j)),
        pl.BlockSpec((None, ck, ck), lambda bb, i, j: (j, 0, 0)),
        pl.BlockSpec((1, ck), lambda bb, i, j: (0, j)),
        pl.BlockSpec((ck, d), lambda bb, i, j: (j, 0)),
        pl.BlockSpec((ck, d), lambda bb, i, j: (nj + j, 0)),
        _full((nj, HIST_CONV3, ck)),
        _full((nj, HIST_POOL, ck)),
    ]
    return _call(
        functools.partial(_mix0_body, tm, pos0),
        (b, s // tm, nj), in_specs,
        (h, gpre, gpost, w_in, w_in, w_in, w_in, conv_w, pool_w, pool_scale, w_out, w_out,
         hist_c, hist_v),
        jax.ShapeDtypeStruct((b, s, d), jnp.float32), pl.BlockSpec((None, tm, d), row),
        [(nj, HIST_CONV3, ck), (nj, HIST_POOL, ck)],
        [pltpu.VMEM((tm, d), jnp.bfloat16)], emit_state, casts, f"mix0_tm{tm}")


def _ffn_body(tm, nj, ins, out_ref, carries, scratch):
    (h_ref, gpre_ref, gpost_ref, wg_ref, wv_ref, cwg_ref, cwv_ref, wd_ref,
     hg_ref, hv_ref) = ins
    carg_ref, carv_ref = carries
    z_ref, extg_ref, extv_ref, act_ref = scratch
    i = pl.program_id(1)
    j = pl.program_id(2)
    _start_tile(i, j, h_ref, gpre_ref, z_ref, out_ref,
                ((hg_ref, carg_ref), (hv_ref, carv_ref)))
    z = z_ref[...]
    _stage_slabs(extg_ref, carg_ref.at[j], _dot(z, wg_ref[...]), HIST_CONV3, tm)
    _stage_slabs(extv_ref, carv_ref.at[j], _dot(z, wv_ref[...]), HIST_CONV3, tm)
    rows = min(GATE_ROWS, tm)
    for c in range(extg_ref.shape[0]):
        for r in range(0, tm, rows):
            g = _taps_block(extg_ref, c, cwg_ref, HIST_CONV3, r, rows, FFN_CONV)
            v = _taps_block(extv_ref, c, cwv_ref, HIST_CONV3, r, rows, FFN_CONV)
            act_ref[pl.ds(r, rows), pl.ds(c * LANES, LANES)] = (
                g * _sigmoid(g) * v).astype(jnp.bfloat16)
    out_ref[...] += _dot(act_ref[...], wd_ref[...])

    @pl.when(j == nj - 1)
    def _():
        _finish_tile(h_ref, gpost_ref, out_ref)


def _ffn_call(h, gpre, gpost, w_up, conv_w, w_down, hist_g, hist_v, *, layer, tm, emit_state,
              casts=()):
    b, s, d = h.shape
    dff = w_down.shape[0]
    tf = FFN_CHUNK
    nj = dff // tf
    row = lambda bb, i, j: (bb, i, 0)
    up = lambda j: jnp.minimum(j, nj - 1)
    in_specs = [
        pl.BlockSpec((None, tm, d), row),
        _full((1, d)),
        _full((1, d)),
        pl.BlockSpec((d, tf), lambda bb, i, j: (0, up(j))),
        pl.BlockSpec((d, tf), lambda bb, i, j: (0, nj + up(j))),
        pl.BlockSpec((None, FFN_CONV, tf), lambda bb, i, j: (layer, 0, up(j))),
        pl.BlockSpec((None, FFN_CONV, tf), lambda bb, i, j: (layer, 0, nj + up(j))),
        pl.BlockSpec((tf, d), lambda bb, i, j: (j, 0)),
        _full((nj, HIST_CONV3, tf)),
        _full((nj, HIST_CONV3, tf)),
    ]
    return _call(
        functools.partial(_ffn_body, tm, nj),
        (b, s // tm, nj), in_specs,
        (h, gpre, gpost, w_up, w_up, conv_w, conv_w, w_down, hist_g, hist_v),
        jax.ShapeDtypeStruct((b, s, d), jnp.float32), pl.BlockSpec((None, tm, d), row),
        [(nj, HIST_CONV3, tf)] * 2,
        [pltpu.VMEM((tm, d), jnp.bfloat16)]
        + [pltpu.VMEM((tf // LANES, HIST_CONV3 + tm, LANES), jnp.float32)] * 2
        + [pltpu.VMEM((tm, tf), jnp.bfloat16)],
        emit_state, casts, f"ffn{layer}_tm{tm}")


def _conf_body(tm, n1, ins, out_ref, carries, scratch):
    (h_ref, gpre_ref, gpost_ref, wa_ref, wg_ref, ba_ref, bg_ref, wdw_ref, bdw_ref,
     lng_ref, lnb_ref, w2_ref, b2_ref, hist_ref) = ins
    (car_ref,) = carries
    z_ref, cbuf_ref, mean_ref, rstd_ref, ext_ref = scratch
    i = pl.program_id(1)
    j = pl.program_id(2)
    nj = pl.num_programs(2)
    cw = CONF_CHUNK
    _start_tile(i, j, h_ref, gpre_ref, z_ref, None, ((hist_ref, car_ref),))

    @pl.when(j < n1)
    def _():
        z = z_ref[...]
        per_pass = CONF_PASS // LANES
        for p in range(cw // CONF_PASS):
            cols = pl.ds(p * CONF_PASS, CONF_PASS)
            a = _dot(z, wa_ref[:, cols]) + ba_ref[:, cols]
            g = _dot(z, wg_ref[:, cols]) + bg_ref[:, cols]
            _stage_slabs(ext_ref, car_ref.at[j], a * _sigmoid(g), HIST_CONV31, tm,
                         first=p * per_pass)
            _taps_from_slabs(ext_ref, wdw_ref, bdw_ref, cbuf_ref.at[j], HIST_CONV31, tm,
                             C_CONV, slabs=range(p * per_pass, (p + 1) * per_pass))

    @pl.when(j == n1)
    def _():
        total = None
        for q in range(n1):
            part = jnp.sum(cbuf_ref[q], axis=-1, keepdims=True)
            total = part if total is None else total + part
        mean = total / (n1 * cw)
        total = None
        for q in range(n1):
            dev = cbuf_ref[q] - mean
            part = jnp.sum(dev * dev, axis=-1, keepdims=True)
            total = part if total is None else total + part
        mean_ref[...] = mean
        rstd_ref[...] = lax.rsqrt(total / (n1 * cw) + EPS)
        out_ref[...] = jnp.broadcast_to(b2_ref[...], out_ref.shape)

    @pl.when(j >= n1)
    def _():
        y = (cbuf_ref[j - n1] - mean_ref[...]) * rstd_ref[...] * lng_ref[...] + lnb_ref[...]
        s = (y * _sigmoid(y)).astype(jnp.bfloat16)
        out_ref[...] += _dot(s, w2_ref[...])

    @pl.when(j == nj - 1)
    def _():
        _finish_tile(h_ref, gpost_ref, out_ref)


def _conf_call(h, gpre, gpost, w1, b1, wdw, bdw, lng, lnb, w2, b2, hist,
               *, tm, emit_state, casts=()):
    b, s, d = h.shape
    c = w2.shape[0]
    cw = CONF_CHUNK
    n1 = c // cw
    row = lambda bb, i, j: (bb, i, 0)
    acol = lambda bb, i, j: (0, jnp.minimum(j, n1 - 1))
    gcol = lambda bb, i, j: (0, n1 + jnp.minimum(j, n1 - 1))
    kcol = lambda bb, i, j: (0, jnp.maximum(j - n1, 0))
    in_specs = [
        pl.BlockSpec((None, tm, d), row),
        _full((1, d)),
        _full((1, d)),
        pl.BlockSpec((d, cw), acol),
        pl.BlockSpec((d, cw), gcol),
        pl.BlockSpec((1, cw), acol),
        pl.BlockSpec((1, cw), gcol),
        pl.BlockSpec((C_CONV, cw), acol),
        pl.BlockSpec((1, cw), acol),
        pl.BlockSpec((1, cw), kcol),
        pl.BlockSpec((1, cw), kcol),
        pl.BlockSpec((cw, d), lambda bb, i, j: (jnp.maximum(j - n1, 0), 0)),
        _full((1, d)),
        _full((n1, HIST_CONV31, cw)),
    ]
    scratch = [
        pltpu.VMEM((tm, d), jnp.bfloat16),
        pltpu.VMEM((n1, tm, cw), jnp.float32),
        pltpu.VMEM((tm, 1), jnp.float32),
        pltpu.VMEM((tm, 1), jnp.float32),
        pltpu.VMEM((cw // LANES, HIST_CONV31 + tm, LANES), jnp.float32),
    ]
    return _call(
        functools.partial(_conf_body, tm, n1),
        (b, s // tm, 2 * n1), in_specs,
        (h, gpre, gpost, w1, w1, b1, b1, wdw, bdw, lng, lnb, w2, b2, hist),
        jax.ShapeDtypeStruct((b, s, d), jnp.float32), pl.BlockSpec((None, tm, d), row),
        [(n1, HIST_CONV31, cw)], scratch, emit_state, casts, f"conf_tm{tm}")


def kernel(x, meta_tokens, mix_pre_g, mix_post_g, ffn_pre_g, ffn_post_g, ab_w_in, ab_conv_w, ab_pool_w, ab_pool_scale, ab_w_out, c_w_pw1, c_b_pw1, c_w_dw, c_b_dw, c_ln_g, c_ln_b, c_w_pw2, c_b_pw2, ffn_w_up, ffn_conv_w, ffn_w_down):
    depth, d = mix_pre_g.shape
    dff = ffn_w_down.shape[1]
    bf = jnp.bfloat16
    f32 = jnp.float32
    assert depth == 2 and x.shape[1] % TM_MAIN == 0

    def zeros(hp, width, chunk):
        return jnp.zeros((width // chunk, hp, chunk), f32)

    def ffn(hm, hx, layer, w_up, w_down, casts):
        fargs = (ffn_pre_g[layer][None], ffn_post_g[layer][None], w_up, ffn_conv_w, w_down)
        zero_u = zeros(HIST_CONV3, dff, FFN_CHUNK)
        hm, st_g, st_v = _ffn_call(hm, *fargs, zero_u, zero_u,
                                   layer=layer, tm=N_META, emit_state=True)
        hx, *cast = _ffn_call(hx, *fargs, st_g, st_v, layer=layer, tm=TM_MAIN,
                              emit_state=False, casts=casts)
        return hm, hx, cast

    hm = meta_tokens.astype(x.dtype)[None]

    args = (mix_pre_g[0][None], mix_post_g[0][None], ab_w_in[0].astype(bf), ab_conv_w[0],
            ab_pool_w[0].astype(bf), ab_pool_scale[0][None], ab_w_out[0].astype(bf))
    hm, st_c, st_v = _mix0_call(
        hm, *args, zeros(HIST_CONV3, A_WIDTH, MIX_CHUNK), zeros(HIST_POOL, B_WIDTH, MIX_CHUNK),
        tm=N_META, pos0=0, emit_state=True)
    hx, w_up0, w_down0 = _mix0_call(
        x, *args, st_c, st_v, tm=TM_MAIN, pos0=N_META, emit_state=False,
        casts=(_cast_job(ffn_w_up, 0, 32), _cast_job(ffn_w_down, 0, 128)))

    hm, hx, (w1, w2, w_up1, w_down1) = ffn(
        hm, hx, 0, w_up0, w_down0,
        (_cast_job(c_w_pw1, 0, 16), _cast_job(c_w_pw2, 0, 16),
         _cast_job(ffn_w_up, 1, 16), _cast_job(ffn_w_down, 1, 32)))

    args = (mix_pre_g[1][None], mix_post_g[1][None], w1, c_b_pw1[0][None], c_w_dw[0],
            c_b_dw[0][None], c_ln_g[0][None], c_ln_b[0][None], w2, c_b_pw2[0][None])
    hm, st = _conf_call(hm, *args, zeros(HIST_CONV31, c_w_pw2.shape[1], CONF_CHUNK),
                        tm=N_META, emit_state=True)
    (hx,) = _conf_call(hx, *args, st, tm=TM_MAIN, emit_state=False)

    _, hx, _ = ffn(hm, hx, 1, w_up1, w_down1, ())
    return hx
```

```python
import functools

import jax
import jax.numpy as jnp
from jax import lax
from jax.experimental import pallas as pl
from jax.experimental.pallas import tpu as pltpu

EPS = 1e-6
N_META = 16
A_WIDTH = 1024
B_WIDTH = 1024
B_WINDOWS = (2, 4, 8, 16)
C_CONV = 31
A_CONV = 3
FFN_CONV = 3

SUBLANES = 8
LANES = 128
TAP_ROWS = 32
GATE_ROWS = 64
NORM_ROWS = 16
HIST_CONV3 = 8
HIST_POOL = 16
HIST_CONV31 = 32

TM_MAIN = 512
TM_FFN = 1024
MIX_CHUNK = 256
FFN_CHUNK = 512
CONF_CHUNK = 512
CONF_PASS = 256

VMEM_LIMIT_BYTES = 60 * 1024 * 1024


def _rms(x, g):
    ms = jnp.mean(x * x, axis=-1, keepdims=True)
    return x * lax.rsqrt(ms + EPS) * g


def _sigmoid(x):
    return 0.5 * jnp.tanh(0.5 * x) + 0.5


def _dot(a, b):
    return jnp.dot(a, b, preferred_element_type=jnp.float32)


def _with_history(carry_ref, cur, hp, tm):
    ext = jnp.concatenate([carry_ref[...], cur], axis=0)
    carry_ref[...] = ext[tm:tm + hp]
    return ext


def _causal_taps(ext, w_ref, hp, tm, k):
    delayed = [ext] + [pltpu.roll(ext, b, axis=0) for b in range(1, min(k, SUBLANES))]
    acc = None
    for q in range(k):
        a, b = divmod(k - 1 - q, SUBLANES)
        lo = hp - SUBLANES * a
        term = w_ref[pl.ds(q, 1), :] * delayed[b][lo:lo + tm]
        acc = term if acc is None else acc + term
    return acc


def _stage_slabs(ext_ref, carry_ref, cur, hp, tm, first=0):
    for k in range(cur.shape[1] // LANES):
        c = first + k
        cols = pl.ds(c * LANES, LANES)
        ext_ref[c, pl.ds(0, hp), :] = carry_ref[:, cols]
        ext_ref[c, pl.ds(hp, tm), :] = cur[:, k * LANES:(k + 1) * LANES]
        carry_ref[:, cols] = ext_ref[c, pl.ds(tm, hp), :]


def _taps_from_slabs(ext_ref, w_ref, b_ref, dst_ref, hp, tm, k, slabs=None):
    rows = min(TAP_ROWS, tm)
    for c in (range(ext_ref.shape[0]) if slabs is None else slabs):
        cols = pl.ds(c * LANES, LANES)
        for r in range(0, tm, rows):
            dst_ref[pl.ds(r, rows), cols] = (b_ref[:, cols]
                                             + _taps_block(ext_ref, c, w_ref, hp, r, rows, k))


def _taps_block(ext_ref, c, w_ref, hp, r, rows, k):
    cols = pl.ds(c * LANES, LANES)
    acc = None
    for q in range(k):
        window = ext_ref[c, pl.ds(hp + r - (k - 1 - q), rows, stride=1), :]
        term = w_ref[pl.ds(q, 1), cols] * window
        acc = term if acc is None else acc + term
    return acc


def _row_blocks(ref):
    rows = min(NORM_ROWS, ref.shape[0])
    return [pl.ds(r, rows) for r in range(0, ref.shape[0], rows)]


def _start_tile(i, j, h_ref, gpre_ref, z_ref, out_ref, carries):
    @pl.when(j == 0)
    def _():
        z_ref[...] = _rms(h_ref[...], gpre_ref[...]).astype(jnp.bfloat16)
        if out_ref is not None:
            out_ref[...] = jnp.zeros_like(out_ref)

    @pl.when(jnp.logical_and(i == 0, j == 0))
    def _():
        for hist_ref, carry_ref in carries:
            carry_ref[...] = hist_ref[...]


def _finish_tile(h_ref, gpost_ref, out_ref):
    for rows in _row_blocks(out_ref):
        out_ref[rows, :] = h_ref[rows, :] + _rms(out_ref[rows, :], gpost_ref[...])


def _flat_step():
    return ((pl.program_id(0) * pl.num_programs(1) + pl.program_id(1)) * pl.num_programs(2)
            + pl.program_id(2))


def _entry(body, n_in, cast_steps, n_carry, emit_state, *refs):
    n_cast = len(cast_steps)
    ins = refs[:n_in]
    cast_src = refs[n_in:n_in + n_cast]
    o = n_in + n_cast
    out_ref = refs[o]
    cast_dst = refs[o + 1:o + 1 + n_cast]
    o += 1 + n_cast
    if emit_state:
        carries, scratch = refs[o:o + n_carry], refs[o + n_carry:]
    else:
        scratch, carries = refs[o:len(refs) - n_carry], refs[len(refs) - n_carry:]
    step = _flat_step()
    for src, dst, n in zip(cast_src, cast_dst, cast_steps):
        @pl.when(step < n)
        def _(src=src, dst=dst):
            dst[...] = src[...].astype(jnp.bfloat16)
    body(ins, out_ref, carries, scratch)


def _cast_job(w, lead, rows):
    shape = w.shape if lead is None else w.shape[1:]
    n = shape[0] // rows
    assert n * rows == shape[0]

    def make(prefix, block):
        def index(bb, i, j, grid):
            step = (bb * grid[1] + i) * grid[2] + j
            return prefix + (jnp.minimum(step, n - 1), 0)
        return block, index

    src = make(() if lead is None else (lead,), ((rows, shape[1]) if lead is None
                                                 else (None, rows, shape[1])))
    dst = make((), (rows, shape[1]))
    return w, src, jax.ShapeDtypeStruct(shape, jnp.bfloat16), dst, n


def _call(body, grid, in_specs, args, out_struct, out_spec, carry_shapes, scratch,
          emit_state, casts, name):
    in_specs, args = list(in_specs), list(args)
    out_shape, out_specs, scratch = [out_struct], [out_spec], list(scratch)
    n_in = len(args)
    for w, (sblk, sidx), struct, (dblk, didx), n_steps in casts:
        assert grid[0] * grid[1] * grid[2] >= n_steps
        args.append(w)
        in_specs.append(pl.BlockSpec(sblk, functools.partial(sidx, grid=grid)))
        out_shape.append(struct)
        out_specs.append(pl.BlockSpec(dblk, functools.partial(didx, grid=grid)))
    if emit_state:
        for shp in carry_shapes:
            out_shape.append(jax.ShapeDtypeStruct(shp, jnp.float32))
            out_specs.append(_full(shp))
    else:
        scratch += [pltpu.VMEM(shp, jnp.float32) for shp in carry_shapes]
    return pl.pallas_call(
        functools.partial(_entry, body, n_in, tuple(c[4] for c in casts), len(carry_shapes),
                          emit_state),
        grid=grid, in_specs=in_specs, out_specs=out_specs, out_shape=out_shape,
        scratch_shapes=scratch,
        compiler_params=pltpu.CompilerParams(
            dimension_semantics=("arbitrary", "arbitrary", "arbitrary"),
            vmem_limit_bytes=VMEM_LIMIT_BYTES),
        name=name,
    )(*args)


def _full(shape):
    return pl.BlockSpec(shape, lambda bb, i, j, n=len(shape): (0,) * n)


def _mix0_body(tm, pos0, ins, out_ref, carries, scratch):
    (h_ref, gpre_ref, gpost_ref, wgb_ref, wgc_ref, wva_ref, wvb_ref, cw_ref,
     pw_ref, ps_ref, woa_ref, wob_ref, hc_ref, hv_ref) = ins
    carc_ref, carv_ref = carries
    (z_ref,) = scratch
    i = pl.program_id(1)
    j = pl.program_id(2)
    nj = pl.num_programs(2)
    _start_tile(i, j, h_ref, gpre_ref, z_ref, out_ref,
                ((hc_ref, carc_ref), (hv_ref, carv_ref)))
    z = z_ref[...]

    c = _dot(z, wgc_ref[...]) * _dot(z, wva_ref[...])
    ext_c = _with_history(carc_ref.at[j], c, HIST_CONV3, tm)
    ya = _dot(z, wgb_ref[...]) * _causal_taps(ext_c, cw_ref, HIST_CONV3, tm, A_CONV)

    vb = _dot(z, wvb_ref[...])
    s = _with_history(carv_ref.at[j], vb, HIST_POOL, tm)
    sums = []
    for step in (1, 2, 4, 8):
        s = s + pltpu.roll(s, step, axis=0)
        sums.append(s[HIST_POOL:HIST_POOL + tm])
    wsum = jnp.where(j == 0, sums[0],
                     jnp.where(j == 1, sums[1], jnp.where(j == 2, sums[2], sums[3])))
    window = jnp.left_shift(2, j)
    pos = pos0 + i * tm + lax.broadcasted_iota(jnp.int32, (tm, 1), 0)
    count = jnp.minimum(pos + 1, window).astype(jnp.float32)
    pg = wsum / count - vb
    yb = _dot(pg.astype(jnp.bfloat16), pw_ref[...]) * ps_ref[...]

    out_ref[...] += (_dot(ya.astype(jnp.bfloat16), woa_ref[...])
                     + _dot(yb.astype(jnp.bfloat16), wob_ref[...]))

    @pl.when(j == nj - 1)
    def _():
        _finish_tile(h_ref, gpost_ref, out_ref)


def _mix0_call(h, gpre, gpost, w_in, conv_w, pool_w, pool_scale, w_out, hist_c, hist_v,
               *, tm, pos0, emit_state, casts=()):
    b, s, d = h.shape
    ck = MIX_CHUNK
    nj = A_WIDTH // ck
    assert B_WINDOWS == tuple(2 << g for g in range(nj)) and B_WIDTH == nj * ck
    row = lambda bb, i, j: (bb, i, 0)
    in_specs = [
        pl.BlockSpec((None, tm, d), row),
        _full((1, d)),
        _full((1, d)),
        pl.BlockSpec((d, ck), lambda bb, i, j: (0, j)),
        pl.BlockSpec((d, ck), lambda bb, i, j: (0, nj + j)),
        pl.BlockSpec((d, ck), lambda bb, i, j: (0, 2 * nj + j)),
        pl.BlockSpec((d, ck), lambda bb, i, j: (0, 3 * nj + j)),
        pl.BlockSpec((A_CONV, ck), lambda bb, i, j: (0, j)),
        pl.BlockSpec((None, ck, ck), lambda bb, i, j: (j, 0, 0)),
        pl.BlockSpec((1, ck), lambda bb, i, j: (0, j)),
        pl.BlockSpec((ck, d), lambda bb, i, j: (j, 0)),
        pl.BlockSpec((ck, d), lambda bb, i, j: (nj + j, 0)),
        _full((nj, HIST_CONV3, ck)),
        _full((nj, HIST_POOL, ck)),
    ]
    return _call(
        functools.partial(_mix0_body, tm, pos0),
        (b, s // tm, nj), in_specs,
        (h, gpre, gpost, w_in, w_in, w_in, w_in, conv_w, pool_w, pool_scale, w_out, w_out,
         hist_c, hist_v),
        jax.ShapeDtypeStruct((b, s, d), jnp.float32), pl.BlockSpec((None, tm, d), row),
        [(nj, HIST_CONV3, ck), (nj, HIST_POOL, ck)],
        [pltpu.VMEM((tm, d), jnp.bfloat16)], emit_state, casts, f"mix0_tm{tm}")


def _ffn_body(tm, nj, ins, out_ref, carries, scratch):
    (h_ref, gpre_ref, gpost_ref, wg_ref, wv_ref, cwg_ref, cwv_ref, wd_ref,
     hg_ref, hv_ref) = ins
    carg_ref, carv_ref = carries
    z_ref, extg_ref, extv_ref, act_ref = scratch
    i = pl.program_id(1)
    j = pl.program_id(2)
    _start_tile(i, j, h_ref, gpre_ref, z_ref, out_ref,
                ((hg_ref, carg_ref), (hv_ref, carv_ref)))
    z = z_ref[...]
    _stage_slabs(extg_ref, carg_ref.at[j], _dot(z, wg_ref[...]), HIST_CONV3, tm)
    _stage_slabs(extv_ref, carv_ref.at[j], _dot(z, wv_ref[...]), HIST_CONV3, tm)
    rows = min(GATE_ROWS, tm)
    for c in range(extg_ref.shape[0]):
        for r in range(0, tm, rows):
            g = _taps_block(extg_ref, c, cwg_ref, HIST_CONV3, r, rows, FFN_CONV)
            v = _taps_block(extv_ref, c, cwv_ref, HIST_CONV3, r, rows, FFN_CONV)
            act_ref[pl.ds(r, rows), pl.ds(c * LANES, LANES)] = (
                g * _sigmoid(g) * v).astype(jnp.bfloat16)
    out_ref[...] += _dot(act_ref[...], wd_ref[...])

    @pl.when(j == nj - 1)
    def _():
        _finish_tile(h_ref, gpost_ref, out_ref)


def _ffn_call(h, gpre, gpost, w_up, conv_w, w_down, hist_g, hist_v, *, layer, tm, emit_state,
              casts=()):
    b, s, d = h.shape
    dff = w_down.shape[0]
    tf = FFN_CHUNK
    nj = dff // tf
    row = lambda bb, i, j: (bb, i, 0)
    up = lambda j: jnp.minimum(j, nj - 1)
    in_specs = [
        pl.BlockSpec((None, tm, d), row),
        _full((1, d)),
        _full((1, d)),
        pl.BlockSpec((d, tf), lambda bb, i, j: (0, up(j))),
        pl.BlockSpec((d, tf), lambda bb, i, j: (0, nj + up(j))),
        pl.BlockSpec((None, FFN_CONV, tf), lambda bb, i, j: (layer, 0, up(j))),
        pl.BlockSpec((None, FFN_CONV, tf), lambda bb, i, j: (layer, 0, nj + up(j))),
        pl.BlockSpec((tf, d), lambda bb, i, j: (j, 0)),
        _full((nj, HIST_CONV3, tf)),
        _full((nj, HIST_CONV3, tf)),
    ]
    return _call(
        functools.partial(_ffn_body, tm, nj),
        (b, s // tm, nj), in_specs,
        (h, gpre, gpost, w_up, w_up, conv_w, conv_w, w_down, hist_g, hist_v),
        jax.ShapeDtypeStruct((b, s, d), jnp.float32), pl.BlockSpec((None, tm, d), row),
        [(nj, HIST_CONV3, tf)] * 2,
        [pltpu.VMEM((tm, d), jnp.bfloat16)]
        + [pltpu.VMEM((tf // LANES, HIST_CONV3 + tm, LANES), jnp.float32)] * 2
        + [pltpu.VMEM((tm, tf), jnp.bfloat16)],
        emit_state, casts, f"ffn{layer}_tm{tm}")


def _conf_body(tm, n1, ins, out_ref, carries, scratch):
    (h_ref, gpre_ref, gpost_ref, wa_ref, wg_ref, ba_ref, bg_ref, wdw_ref, bdw_ref,
     lng_ref, lnb_ref, w2_ref, b2_ref, hist_ref) = ins
    (car_ref,) = carries
    z_ref, cbuf_ref, mean_ref, rstd_ref, ext_ref = scratch
    i = pl.program_id(1)
    j = pl.program_id(2)
    nj = pl.num_programs(2)
    cw = CONF_CHUNK
    _start_tile(i, j, h_ref, gpre_ref, z_ref, None, ((hist_ref, car_ref),))

    @pl.when(j < n1)
    def _():
        z = z_ref[...]
        per_pass = CONF_PASS // LANES
        for p in range(cw // CONF_PASS):
            cols = pl.ds(p * CONF_PASS, CONF_PASS)
            a = _dot(z, wa_ref[:, cols]) + ba_ref[:, cols]
            g = _dot(z, wg_ref[:, cols]) + bg_ref[:, cols]
            _stage_slabs(ext_ref, car_ref.at[j], a * _sigmoid(g), HIST_CONV31, tm,
                         first=p * per_pass)
            _taps_from_slabs(ext_ref, wdw_ref, bdw_ref, cbuf_ref.at[j], HIST_CONV31, tm,
                             C_CONV, slabs=range(p * per_pass, (p + 1) * per_pass))

    @pl.when(j == n1)
    def _():
        total = None
        for q in range(n1):
            part = jnp.sum(cbuf_ref[q], axis=-1, keepdims=True)
            total = part if total is None else total + part
        mean = total / (n1 * cw)
        total = None
        for q in range(n1):
            dev = cbuf_ref[q] - mean
            part = jnp.sum(dev * dev, axis=-1, keepdims=True)
            total = part if total is None else total + part
        mean_ref[...] = mean
        rstd_ref[...] = lax.rsqrt(total / (n1 * cw) + EPS)
        out_ref[...] = jnp.broadcast_to(b2_ref[...], out_ref.shape)

    @pl.when(j >= n1)
    def _():
        y = (cbuf_ref[j - n1] - mean_ref[...]) * rstd_ref[...] * lng_ref[...] + lnb_ref[...]
        s = (y * _sigmoid(y)).astype(jnp.bfloat16)
        out_ref[...] += _dot(s, w2_ref[...])

    @pl.when(j == nj - 1)
    def _():
        _finish_tile(h_ref, gpost_ref, out_ref)


def _conf_call(h, gpre, gpost, w1, b1, wdw, bdw, lng, lnb, w2, b2, hist,
               *, tm, emit_state, casts=()):
    b, s, d = h.shape
    c = w2.shape[0]
    cw = CONF_CHUNK
    n1 = c // cw
    row = lambda bb, i, j: (bb, i, 0)
    acol = lambda bb, i, j: (0, jnp.minimum(j, n1 - 1))
    gcol = lambda bb, i, j: (0, n1 + jnp.minimum(j, n1 - 1))
    kcol = lambda bb, i, j: (0, jnp.maximum(j - n1, 0))
    in_specs = [
        pl.BlockSpec((None, tm, d), row),
        _full((1, d)),
        _full((1, d)),
        pl.BlockSpec((d, cw), acol),
        pl.BlockSpec((d, cw), gcol),
        pl.BlockSpec((1, cw), acol),
        pl.BlockSpec((1, cw), gcol),
        pl.BlockSpec((C_CONV, cw), acol),
        pl.BlockSpec((1, cw), acol),
        pl.BlockSpec((1, cw), kcol),
        pl.BlockSpec((1, cw), kcol),
        pl.BlockSpec((cw, d), lambda bb, i, j: (jnp.maximum(j - n1, 0), 0)),
        _full((1, d)),
        _full((n1, HIST_CONV31, cw)),
    ]
    scratch = [
        pltpu.VMEM((tm, d), jnp.bfloat16),
        pltpu.VMEM((n1, tm, cw), jnp.float32),
        pltpu.VMEM((tm, 1), jnp.float32),
        pltpu.VMEM((tm, 1), jnp.float32),
        pltpu.VMEM((cw // LANES, HIST_CONV31 + tm, LANES), jnp.float32),
    ]
    return _call(
        functools.partial(_conf_body, tm, n1),
        (b, s // tm, 2 * n1), in_specs,
        (h, gpre, gpost, w1, w1, b1, b1, wdw, bdw, lng, lnb, w2, b2, hist),
        jax.ShapeDtypeStruct((b, s, d), jnp.float32), pl.BlockSpec((None, tm, d), row),
        [(n1, HIST_CONV31, cw)], scratch, emit_state, casts, f"conf_tm{tm}")


def kernel(x, meta_tokens, mix_pre_g, mix_post_g, ffn_pre_g, ffn_post_g, ab_w_in, ab_conv_w, ab_pool_w, ab_pool_scale, ab_w_out, c_w_pw1, c_b_pw1, c_w_dw, c_b_dw, c_ln_g, c_ln_b, c_w_pw2, c_b_pw2, ffn_w_up, ffn_conv_w, ffn_w_down):
    depth, d = mix_pre_g.shape
    dff = ffn_w_down.shape[1]
    bf = jnp.bfloat16
    f32 = jnp.float32
    assert depth == 2 and x.shape[1] % TM_MAIN == 0

    def zeros(hp, width, chunk):
        return jnp.zeros((width // chunk, hp, chunk), f32)

    def ffn(hm, hx, layer, w_up, w_down, casts):
        fargs = (ffn_pre_g[layer][None], ffn_post_g[layer][None], w_up, ffn_conv_w, w_down)
        zero_u = zeros(HIST_CONV3, dff, FFN_CHUNK)
        hm, st_g, st_v = _ffn_call(hm, *fargs, zero_u, zero_u,
                                   layer=layer, tm=N_META, emit_state=True)
        hx, *cast = _ffn_call(hx, *fargs, st_g, st_v, layer=layer, tm=TM_FFN,
                              emit_state=False, casts=casts)
        return hm, hx, cast

    hm = meta_tokens.astype(x.dtype)[None]

    args = (mix_pre_g[0][None], mix_post_g[0][None], ab_w_in[0].astype(bf), ab_conv_w[0],
            ab_pool_w[0].astype(bf), ab_pool_scale[0][None], ab_w_out[0].astype(bf))
    hm, st_c, st_v = _mix0_call(
        hm, *args, zeros(HIST_CONV3, A_WIDTH, MIX_CHUNK), zeros(HIST_POOL, B_WIDTH, MIX_CHUNK),
        tm=N_META, pos0=0, emit_state=True)
    hx, w_up0, w_down0, w1, w2 = _mix0_call(
        x, *args, st_c, st_v, tm=TM_MAIN, pos0=N_META, emit_state=False,
        casts=(_cast_job(ffn_w_up, 0, 32), _cast_job(ffn_w_down, 0, 128),
               _cast_job(c_w_pw1, 0, 32), _cast_job(c_w_pw2, 0, 32)))

    hm, hx, _ = ffn(hm, hx, 0, w_up0, w_down0, ())

    args = (mix_pre_g[1][None], mix_post_g[1][None], w1, c_b_pw1[0][None], c_w_dw[0],
            c_b_dw[0][None], c_ln_g[0][None], c_ln_b[0][None], w2, c_b_pw2[0][None])
    hm, st = _conf_call(hm, *args, zeros(HIST_CONV31, c_w_pw2.shape[1], CONF_CHUNK),
                        tm=N_META, emit_state=True)
    hx, w_up1, w_down1 = _conf_call(
        hx, *args, st, tm=TM_MAIN, emit_state=False,
        casts=(_cast_job(ffn_w_up, 1, 16), _cast_job(ffn_w_down, 1, 64)))

    _, hx, _ = ffn(hm, hx, 1, w_up1, w_down1, ())
    return hx
```

```python
import functools

import jax
import jax.numpy as jnp
from jax import lax
from jax.experimental import pallas as pl
from jax.experimental.pallas import tpu as pltpu

EPS = 1e-6
N_META = 16
A_WIDTH = 1024
B_WIDTH = 1024
B_WINDOWS = (2, 4, 8, 16)
C_CONV = 31
A_CONV = 3
FFN_CONV = 3

SUBLANES = 8
LANES = 128
TAP_ROWS = 32
GATE_ROWS = 64
NORM_ROWS = 16
HIST_CONV3 = 8
HIST_POOL = 16
HIST_CONV31 = 32

TM_MAIN = 512
TM_FFN = 1024
MIX_CHUNK = 256
FFN_CHUNK = 512
CONF_CHUNK = 512
CONF_PASS = 256
CONF_OUT_CHUNK = 2048

VMEM_LIMIT_BYTES = 60 * 1024 * 1024


def _rms(x, g):
    ms = jnp.mean(x * x, axis=-1, keepdims=True)
    return x * lax.rsqrt(ms + EPS) * g


def _sigmoid(x):
    return 0.5 * jnp.tanh(0.5 * x) + 0.5


def _dot(a, b):
    return jnp.dot(a, b, preferred_element_type=jnp.float32)


def _with_history(carry_ref, cur, hp, tm):
    ext = jnp.concatenate([carry_ref[...], cur], axis=0)
    carry_ref[...] = ext[tm:tm + hp]
    return ext


def _causal_taps(ext, w_ref, hp, tm, k):
    delayed = [ext] + [pltpu.roll(ext, b, axis=0) for b in range(1, min(k, SUBLANES))]
    acc = None
    for q in range(k):
        a, b = divmod(k - 1 - q, SUBLANES)
        lo = hp - SUBLANES * a
        term = w_ref[pl.ds(q, 1), :] * delayed[b][lo:lo + tm]
        acc = term if acc is None else acc + term
    return acc


def _stage_slabs(ext_ref, carry_ref, cur, hp, tm, first=0):
    for k in range(cur.shape[1] // LANES):
        c = first + k
        cols = pl.ds(c * LANES, LANES)
        ext_ref[c, pl.ds(0, hp), :] = carry_ref[:, cols]
        ext_ref[c, pl.ds(hp, tm), :] = cur[:, k * LANES:(k + 1) * LANES]
        carry_ref[:, cols] = ext_ref[c, pl.ds(tm, hp), :]


def _taps_from_slabs(ext_ref, w_ref, b_ref, dst_ref, hp, tm, k, slabs=None):
    rows = min(TAP_ROWS, tm)
    for c in (range(ext_ref.shape[0]) if slabs is None else slabs):
        cols = pl.ds(c * LANES, LANES)
        for r in range(0, tm, rows):
            dst_ref[pl.ds(r, rows), cols] = (b_ref[:, cols]
                                             + _taps_block(ext_ref, c, w_ref, hp, r, rows, k))


def _taps_block(ext_ref, c, w_ref, hp, r, rows, k):
    cols = pl.ds(c * LANES, LANES)
    acc = None
    for q in range(k):
        window = ext_ref[c, pl.ds(hp + r - (k - 1 - q), rows, stride=1), :]
        term = w_ref[pl.ds(q, 1), cols] * window
        acc = term if acc is None else acc + term
    return acc


def _row_blocks(ref):
    rows = min(NORM_ROWS, ref.shape[0])
    return [pl.ds(r, rows) for r in range(0, ref.shape[0], rows)]


def _start_tile(i, j, h_ref, gpre_ref, z_ref, out_ref, carries):
    @pl.when(j == 0)
    def _():
        z_ref[...] = _rms(h_ref[...], gpre_ref[...]).astype(jnp.bfloat16)
        if out_ref is not None:
            out_ref[...] = jnp.zeros_like(out_ref)

    @pl.when(jnp.logical_and(i == 0, j == 0))
    def _():
        for hist_ref, carry_ref in carries:
            carry_ref[...] = hist_ref[...]


def _finish_tile(h_ref, gpost_ref, out_ref):
    for rows in _row_blocks(out_ref):
        out_ref[rows, :] = h_ref[rows, :] + _rms(out_ref[rows, :], gpost_ref[...])


def _flat_step():
    return ((pl.program_id(0) * pl.num_programs(1) + pl.program_id(1)) * pl.num_programs(2)
            + pl.program_id(2))


def _entry(body, n_in, cast_steps, n_carry, emit_state, *refs):
    n_cast = len(cast_steps)
    ins = refs[:n_in]
    cast_src = refs[n_in:n_in + n_cast]
    o = n_in + n_cast
    out_ref = refs[o]
    cast_dst = refs[o + 1:o + 1 + n_cast]
    o += 1 + n_cast
    if emit_state:
        carries, scratch = refs[o:o + n_carry], refs[o + n_carry:]
    else:
        scratch, carries = refs[o:len(refs) - n_carry], refs[len(refs) - n_carry:]
    step = _flat_step()
    for src, dst, n in zip(cast_src, cast_dst, cast_steps):
        @pl.when(step < n)
        def _(src=src, dst=dst):
            dst[...] = src[...].astype(jnp.bfloat16)
    body(ins, out_ref, carries, scratch)


def _cast_job(w, lead, rows):
    shape = w.shape if lead is None else w.shape[1:]
    n = shape[0] // rows
    assert n * rows == shape[0]

    def make(prefix, block):
        def index(bb, i, j, grid):
            step = (bb * grid[1] + i) * grid[2] + j
            return prefix + (jnp.minimum(step, n - 1), 0)
        return block, index

    src = make(() if lead is None else (lead,), ((rows, shape[1]) if lead is None
                                                 else (None, rows, shape[1])))
    dst = make((), (rows, shape[1]))
    return w, src, jax.ShapeDtypeStruct(shape, jnp.bfloat16), dst, n


def _call(body, grid, in_specs, args, out_struct, out_spec, carry_shapes, scratch,
          emit_state, casts, name):
    in_specs, args = list(in_specs), list(args)
    out_shape, out_specs, scratch = [out_struct], [out_spec], list(scratch)
    n_in = len(args)
    for w, (sblk, sidx), struct, (dblk, didx), n_steps in casts:
        assert grid[0] * grid[1] * grid[2] >= n_steps
        args.append(w)
        in_specs.append(pl.BlockSpec(sblk, functools.partial(sidx, grid=grid)))
        out_shape.append(struct)
        out_specs.append(pl.BlockSpec(dblk, functools.partial(didx, grid=grid)))
    if emit_state:
        for shp in carry_shapes:
            out_shape.append(jax.ShapeDtypeStruct(shp, jnp.float32))
            out_specs.append(_full(shp))
    else:
        scratch += [pltpu.VMEM(shp, jnp.float32) for shp in carry_shapes]
    return pl.pallas_call(
        functools.partial(_entry, body, n_in, tuple(c[4] for c in casts), len(carry_shapes),
                          emit_state),
        grid=grid, in_specs=in_specs, out_specs=out_specs, out_shape=out_shape,
        scratch_shapes=scratch,
        compiler_params=pltpu.CompilerParams(
            dimension_semantics=("arbitrary", "arbitrary", "arbitrary"),
            vmem_limit_bytes=VMEM_LIMIT_BYTES),
        name=name,
    )(*args)


def _full(shape):
    return pl.BlockSpec(shape, lambda bb, i, j, n=len(shape): (0,) * n)


def _mix0_body(tm, pos0, ins, out_ref, carries, scratch):
    (h_ref, gpre_ref, gpost_ref, wgb_ref, wgc_ref, wva_ref, wvb_ref, cw_ref,
     pw_ref, ps_ref, woa_ref, wob_ref, hc_ref, hv_ref) = ins
    carc_ref, carv_ref = carries
    (z_ref,) = scratch
    i = pl.program_id(1)
    j = pl.program_id(2)
    nj = pl.num_programs(2)
    _start_tile(i, j, h_ref, gpre_ref, z_ref, out_ref,
                ((hc_ref, carc_ref), (hv_ref, carv_ref)))
    z = z_ref[...]

    c = _dot(z, wgc_ref[...]) * _dot(z, wva_ref[...])
    ext_c = _with_history(carc_ref.at[j], c, HIST_CONV3, tm)
    ya = _dot(z, wgb_ref[...]) * _causal_taps(ext_c, cw_ref, HIST_CONV3, tm, A_CONV)

    vb = _dot(z, wvb_ref[...])
    s = _with_history(carv_ref.at[j], vb, HIST_POOL, tm)
    sums = []
    for step in (1, 2, 4, 8):
        s = s + pltpu.roll(s, step, axis=0)
        sums.append(s[HIST_POOL:HIST_POOL + tm])
    wsum = jnp.where(j == 0, sums[0],
                     jnp.where(j == 1, sums[1], jnp.where(j == 2, sums[2], sums[3])))
    window = jnp.left_shift(2, j)
    pos = pos0 + i * tm + lax.broadcasted_iota(jnp.int32, (tm, 1), 0)
    count = jnp.minimum(pos + 1, window).astype(jnp.float32)
    pg = wsum / count - vb
    yb = _dot(pg.astype(jnp.bfloat16), pw_ref[...]) * ps_ref[...]

    out_ref[...] += (_dot(ya.astype(jnp.bfloat16), woa_ref[...])
                     + _dot(yb.astype(jnp.bfloat16), wob_ref[...]))

    @pl.when(j == nj - 1)
    def _():
        _finish_tile(h_ref, gpost_ref, out_ref)


def _mix0_call(h, gpre, gpost, w_in, conv_w, pool_w, pool_scale, w_out, hist_c, hist_v,
               *, tm, pos0, emit_state, casts=()):
    b, s, d = h.shape
    ck = MIX_CHUNK
    nj = A_WIDTH // ck
    assert B_WINDOWS == tuple(2 << g for g in range(nj)) and B_WIDTH == nj * ck
    row = lambda bb, i, j: (bb, i, 0)
    in_specs = [
        pl.BlockSpec((None, tm, d), row),
        _full((1, d)),
        _full((1, d)),
        pl.BlockSpec((d, ck), lambda bb, i, j: (0, j)),
        pl.BlockSpec((d, ck), lambda bb, i, j: (0, nj + j)),
        pl.BlockSpec((d, ck), lambda bb, i, j: (0, 2 * nj + j)),
        pl.BlockSpec((d, ck), lambda bb, i, j: (0, 3 * nj + j)),
        pl.BlockSpec((A_CONV, ck), lambda bb, i, j: (0, j)),
        pl.BlockSpec((None, ck, ck), lambda bb, i, j: (j, 0, 0)),
        pl.BlockSpec((1, ck), lambda bb, i, j: (0, j)),
        pl.BlockSpec((ck, d), lambda bb, i, j: (j, 0)),
        pl.BlockSpec((ck, d), lambda bb, i, j: (nj + j, 0)),
        _full((nj, HIST_CONV3, ck)),
        _full((nj, HIST_POOL, ck)),
    ]
    return _call(
        functools.partial(_mix0_body, tm, pos0),
        (b, s // tm, nj), in_specs,
        (h, gpre, gpost, w_in, w_in, w_in, w_in, conv_w, pool_w, pool_scale, w_out, w_out,
         hist_c, hist_v),
        jax.ShapeDtypeStruct((b, s, d), jnp.float32), pl.BlockSpec((None, tm, d), row),
        [(nj, HIST_CONV3, ck), (nj, HIST_POOL, ck)],
        [pltpu.VMEM((tm, d), jnp.bfloat16)], emit_state, casts, f"mix0_tm{tm}")


def _ffn_body(tm, nj, ins, out_ref, carries, scratch):
    (h_ref, gpre_ref, gpost_ref, wg_ref, wv_ref, cwg_ref, cwv_ref, wd_ref,
     hg_ref, hv_ref) = ins
    carg_ref, carv_ref = carries
    z_ref, extg_ref, extv_ref, act_ref = scratch
    i = pl.program_id(1)
    j = pl.program_id(2)
    _start_tile(i, j, h_ref, gpre_ref, z_ref, out_ref,
                ((hg_ref, carg_ref), (hv_ref, carv_ref)))
    z = z_ref[...]
    _stage_slabs(extg_ref, carg_ref.at[j], _dot(z, wg_ref[...]), HIST_CONV3, tm)
    _stage_slabs(extv_ref, carv_ref.at[j], _dot(z, wv_ref[...]), HIST_CONV3, tm)
    rows = min(GATE_ROWS, tm)
    for c in range(extg_ref.shape[0]):
        for r in range(0, tm, rows):
            g = _taps_block(extg_ref, c, cwg_ref, HIST_CONV3, r, rows, FFN_CONV)
            v = _taps_block(extv_ref, c, cwv_ref, HIST_CONV3, r, rows, FFN_CONV)
            act_ref[pl.ds(r, rows), pl.ds(c * LANES, LANES)] = (
                g * _sigmoid(g) * v).astype(jnp.bfloat16)
    out_ref[...] += _dot(act_ref[...], wd_ref[...])

    @pl.when(j == nj - 1)
    def _():
        _finish_tile(h_ref, gpost_ref, out_ref)


def _ffn_call(h, gpre, gpost, w_up, conv_w, w_down, hist_g, hist_v, *, layer, tm, emit_state,
              casts=()):
    b, s, d = h.shape
    dff = w_down.shape[0]
    tf = FFN_CHUNK
    nj = dff // tf
    row = lambda bb, i, j: (bb, i, 0)
    up = lambda j: jnp.minimum(j, nj - 1)
    in_specs = [
        pl.BlockSpec((None, tm, d), row),
        _full((1, d)),
        _full((1, d)),
        pl.BlockSpec((d, tf), lambda bb, i, j: (0, up(j))),
        pl.BlockSpec((d, tf), lambda bb, i, j: (0, nj + up(j))),
        pl.BlockSpec((None, FFN_CONV, tf), lambda bb, i, j: (layer, 0, up(j))),
        pl.BlockSpec((None, FFN_CONV, tf), lambda bb, i, j: (layer, 0, nj + up(j))),
        pl.BlockSpec((tf, d), lambda bb, i, j: (j, 0)),
        _full((nj, HIST_CONV3, tf)),
        _full((nj, HIST_CONV3, tf)),
    ]
    return _call(
        functools.partial(_ffn_body, tm, nj),
        (b, s // tm, nj), in_specs,
        (h, gpre, gpost, w_up, w_up, conv_w, conv_w, w_down, hist_g, hist_v),
        jax.ShapeDtypeStruct((b, s, d), jnp.float32), pl.BlockSpec((None, tm, d), row),
        [(nj, HIST_CONV3, tf)] * 2,
        [pltpu.VMEM((tm, d), jnp.bfloat16)]
        + [pltpu.VMEM((tf // LANES, HIST_CONV3 + tm, LANES), jnp.float32)] * 2
        + [pltpu.VMEM((tm, tf), jnp.bfloat16)],
        emit_state, casts, f"ffn{layer}_tm{tm}")


def _conf_body(tm, n1, ins, out_ref, carries, scratch):
    (h_ref, gpre_ref, gpost_ref, wa_ref, wg_ref, ba_ref, bg_ref, wdw_ref, bdw_ref,
     lng_ref, lnb_ref, w2_ref, b2_ref, hist_ref) = ins
    (car_ref,) = carries
    z_ref, cbuf_ref, mean_ref, rstd_ref, ext_ref = scratch
    i = pl.program_id(1)
    j = pl.program_id(2)
    nj = pl.num_programs(2)
    cw = CONF_CHUNK
    _start_tile(i, j, h_ref, gpre_ref, z_ref, None, ((hist_ref, car_ref),))

    @pl.when(j < n1)
    def _():
        z = z_ref[...]
        per_pass = CONF_PASS // LANES
        for p in range(cw // CONF_PASS):
            cols = pl.ds(p * CONF_PASS, CONF_PASS)
            a = _dot(z, wa_ref[:, cols]) + ba_ref[:, cols]
            g = _dot(z, wg_ref[:, cols]) + bg_ref[:, cols]
            _stage_slabs(ext_ref, car_ref.at[j], a * _sigmoid(g), HIST_CONV31, tm,
                         first=p * per_pass)
            _taps_from_slabs(ext_ref, wdw_ref, bdw_ref, cbuf_ref.at[j], HIST_CONV31, tm,
                             C_CONV, slabs=range(p * per_pass, (p + 1) * per_pass))

    @pl.when(j == n1)
    def _():
        total = None
        for q in range(n1):
            part = jnp.sum(cbuf_ref[q], axis=-1, keepdims=True)
            total = part if total is None else total + part
        mean = total / (n1 * cw)
        total = None
        for q in range(n1):
            dev = cbuf_ref[q] - mean
            part = jnp.sum(dev * dev, axis=-1, keepdims=True)
            total = part if total is None else total + part
        mean_ref[...] = mean
        rstd_ref[...] = lax.rsqrt(total / (n1 * cw) + EPS)
        out_ref[...] = jnp.broadcast_to(b2_ref[...], out_ref.shape)

    @pl.when(j >= n1)
    def _():
        per_out = CONF_OUT_CHUNK // cw
        parts = []
        for t in range(per_out):
            cols = pl.ds(t * cw, cw)
            y = ((cbuf_ref[(j - n1) * per_out + t] - mean_ref[...]) * rstd_ref[...]
                 * lng_ref[:, cols] + lnb_ref[:, cols])
            parts.append((y * _sigmoid(y)).astype(jnp.bfloat16))
        out_ref[...] += _dot(jnp.concatenate(parts, axis=1), w2_ref[...])

    @pl.when(j == nj - 1)
    def _():
        _finish_tile(h_ref, gpost_ref, out_ref)


def _conf_call(h, gpre, gpost, w1, b1, wdw, bdw, lng, lnb, w2, b2, hist,
               *, tm, emit_state, casts=()):
    b, s, d = h.shape
    c = w2.shape[0]
    cw = CONF_CHUNK
    n1 = c // cw
    row = lambda bb, i, j: (bb, i, 0)
    acol = lambda bb, i, j: (0, jnp.minimum(j, n1 - 1))
    gcol = lambda bb, i, j: (0, n1 + jnp.minimum(j, n1 - 1))
    kcol = lambda bb, i, j: (0, jnp.maximum(j - n1, 0))
    ck = CONF_OUT_CHUNK
    in_specs = [
        pl.BlockSpec((None, tm, d), row),
        _full((1, d)),
        _full((1, d)),
        pl.BlockSpec((d, cw), acol),
        pl.BlockSpec((d, cw), gcol),
        pl.BlockSpec((1, cw), acol),
        pl.BlockSpec((1, cw), gcol),
        pl.BlockSpec((C_CONV, cw), acol),
        pl.BlockSpec((1, cw), acol),
        pl.BlockSpec((1, ck), kcol),
        pl.BlockSpec((1, ck), kcol),
        pl.BlockSpec((ck, d), lambda bb, i, j: (jnp.maximum(j - n1, 0), 0)),
        _full((1, d)),
        _full((n1, HIST_CONV31, cw)),
    ]
    scratch = [
        pltpu.VMEM((tm, d), jnp.bfloat16),
        pltpu.VMEM((n1, tm, cw), jnp.float32),
        pltpu.VMEM((tm, 1), jnp.float32),
        pltpu.VMEM((tm, 1), jnp.float32),
        pltpu.VMEM((cw // LANES, HIST_CONV31 + tm, LANES), jnp.float32),
    ]
    return _call(
        functools.partial(_conf_body, tm, n1),
        (b, s // tm, n1 + c // ck), in_specs,
        (h, gpre, gpost, w1, w1, b1, b1, wdw, bdw, lng, lnb, w2, b2, hist),
        jax.ShapeDtypeStruct((b, s, d), jnp.float32), pl.BlockSpec((None, tm, d), row),
        [(n1, HIST_CONV31, cw)], scratch, emit_state, casts, f"conf_tm{tm}")


def kernel(x, meta_tokens, mix_pre_g, mix_post_g, ffn_pre_g, ffn_post_g, ab_w_in, ab_conv_w, ab_pool_w, ab_pool_scale, ab_w_out, c_w_pw1, c_b_pw1, c_w_dw, c_b_dw, c_ln_g, c_ln_b, c_w_pw2, c_b_pw2, ffn_w_up, ffn_conv_w, ffn_w_down):
    depth, d = mix_pre_g.shape
    dff = ffn_w_down.shape[1]
    bf = jnp.bfloat16
    f32 = jnp.float32
    assert depth == 2 and x.shape[1] % TM_MAIN == 0

    def zeros(hp, width, chunk):
        return jnp.zeros((width // chunk, hp, chunk), f32)

    def ffn(hm, hx, layer, w_up, w_down, casts):
        fargs = (ffn_pre_g[layer][None], ffn_post_g[layer][None], w_up, ffn_conv_w, w_down)
        zero_u = zeros(HIST_CONV3, dff, FFN_CHUNK)
        hm, st_g, st_v = _ffn_call(hm, *fargs, zero_u, zero_u,
                                   layer=layer, tm=N_META, emit_state=True)
        hx, *cast = _ffn_call(hx, *fargs, st_g, st_v, layer=layer, tm=TM_FFN,
                              emit_state=False, casts=casts)
        return hm, hx, cast

    hm = meta_tokens.astype(x.dtype)[None]

    args = (mix_pre_g[0][None], mix_post_g[0][None], ab_w_in[0].astype(bf), ab_conv_w[0],
            ab_pool_w[0].astype(bf), ab_pool_scale[0][None], ab_w_out[0].astype(bf))
    hm, st_c, st_v = _mix0_call(
        hm, *args, zeros(HIST_CONV3, A_WIDTH, MIX_CHUNK), zeros(HIST_POOL, B_WIDTH, MIX_CHUNK),
        tm=N_META, pos0=0, emit_state=True)
    hx, w_up0, w_down0, w1, w2 = _mix0_call(
        x, *args, st_c, st_v, tm=TM_MAIN, pos0=N_META, emit_state=False,
        casts=(_cast_job(ffn_w_up, 0, 32), _cast_job(ffn_w_down, 0, 128),
               _cast_job(c_w_pw1, 0, 32), _cast_job(c_w_pw2, 0, 32)))

    hm, hx, _ = ffn(hm, hx, 0, w_up0, w_down0, ())

    args = (mix_pre_g[1][None], mix_post_g[1][None], w1, c_b_pw1[0][None], c_w_dw[0],
            c_b_dw[0][None], c_ln_g[0][None], c_ln_b[0][None], w2, c_b_pw2[0][None])
    hm, st = _conf_call(hm, *args, zeros(HIST_CONV31, c_w_pw2.shape[1], CONF_CHUNK),
                        tm=N_META, emit_state=True)
    hx, w_up1, w_down1 = _conf_call(
        hx, *args, st, tm=TM_MAIN, emit_state=False,
        casts=(_cast_job(ffn_w_up, 1, 32), _cast_job(ffn_w_down, 1, 128)))

    _, hx, _ = ffn(hm, hx, 1, w_up1, w_down1, ())
    return hx
```
